```python
import jax, jax.numpy as jnp
from jax import lax
import numpy as np

D_MODEL = 1024
BATCH = 16
SEQ = 2048
DEPTH = 1

SSM_EXPAND = 2
D_INNER = SSM_EXPAND * D_MODEL
SSM_HEAD_DIM = 64
SSM_HEADS = D_INNER // SSM_HEAD_DIM
SSM_GROUPS = 4
SSM_STATE = 128
CONV_K = 4
CHUNK = 128
XBC_W = D_INNER + 2 * SSM_GROUPS * SSM_STATE
DT_MIN = 0.001
DT_MAX = 0.1

ATT_HEADS = 16
ATT_HEAD_DIM = 64
ATT_KV_HEADS = 4
IDX_HEADS = 8
IDX_DIM = 64
TOPK_MAX = 256
Q_BLOCK = 128
ALIBI_BASE = 8.0

D_FF = 2816
FFN_RESIDUAL_SCALE = 0.5
EPS = 1e-6

IN_SIZES = (D_INNER,
            XBC_W,
            SSM_HEADS,
            ATT_HEADS * ATT_HEAD_DIM,
            ATT_KV_HEADS * ATT_HEAD_DIM,
            ATT_KV_HEADS * ATT_HEAD_DIM,
            IDX_HEADS * IDX_DIM,
            IDX_DIM,
            IDX_HEADS,
            D_MODEL,
            D_MODEL)
IN_W = 2048 + 3072 + 32 + 1024 + 256 + 256 + 512 + 64 + 8 + 1024 + 1024

kernel_name = "hybrid_ssd_dsa_gated_macaron"


def rms_norm(x, g):
    xf = x.astype(jnp.float32)
    y = xf * lax.rsqrt(jnp.mean(xf * xf, axis=-1, keepdims=True) + EPS)
    return (y * g.astype(jnp.float32)).astype(x.dtype)


def swiglu(x, w_gate, w_up, w_down):
    return (jax.nn.silu(x @ w_gate) * (x @ w_up)) @ w_down


def causal_depthwise_conv(x, w, b):
    c = x.shape[-1]
    y = lax.conv_general_dilated(x, w[:, None, :].astype(x.dtype), window_strides=(1,),
                                 padding=[(CONV_K - 1, 0)],
                                 dimension_numbers=("NWC", "WIO", "NWC"),
                                 feature_group_count=c)
    return y + b.astype(x.dtype)


def ssd_chunked(xs, dt, a_head, bm, cm):
    bsz, seq, n_heads, hd = xs.shape
    g, n = bm.shape[2], bm.shape[3]
    r = n_heads // g
    nc = seq // CHUNK
    xdt = (xs.astype(jnp.float32) * dt[..., None]).reshape(bsz, nc, CHUNK, g, r, hd)
    a = (dt * a_head).reshape(bsz, nc, CHUNK, g, r).transpose(0, 1, 3, 4, 2)
    a_cs = jnp.cumsum(a, axis=-1)
    bmc = bm.astype(jnp.float32).reshape(bsz, nc, CHUNK, g, n)
    cmc = cm.astype(jnp.float32).reshape(bsz, nc, CHUNK, g, n)
    causal = jnp.tril(jnp.ones((CHUNK, CHUNK), dtype=bool))
    seg = a_cs[..., :, None] - a_cs[..., None, :]
    decay = jnp.where(causal, jnp.exp(jnp.where(causal, seg, 0.0)), 0.0)
    cb = jnp.einsum("bcqgn,bcsgn->bcgqs", cmc, bmc)
    y_diag = jnp.einsum("bcgrqs,bcsgrp->bcqgrp", cb[:, :, :, None] * decay, xdt)
    decay_states = jnp.exp(a_cs[..., -1:] - a_cs).transpose(0, 1, 4, 2, 3)
    states = jnp.einsum("bcsgn,bcsgrp->bcgrpn", bmc, xdt * decay_states[..., None])
    chunk_decay = jnp.exp(a_cs[..., -1])

    def step(h, inp):
        s_c, d_c = inp
        return h * d_c[..., None, None] + s_c, h

    h0 = jnp.zeros((bsz, g, r, hd, n), jnp.float32)
    _, prev = lax.scan(step, h0, (jnp.moveaxis(states, 1, 0), jnp.moveaxis(chunk_decay, 1, 0)))
    prev = jnp.moveaxis(prev, 0, 1)
    in_decay = jnp.exp(a_cs).transpose(0, 1, 4, 2, 3)
    y_off = jnp.einsum("bcqgn,bcgrpn->bcqgrp", cmc, prev) * in_decay[..., None]
    return (y_diag + y_off).reshape(bsz, seq, n_heads, hd)


def mamba2_branch(z, xbc, dt_raw, conv_w, conv_b, dt_bias, a_log, d_skip, norm_w):
    bsz, seq, _ = z.shape
    xbc = jax.nn.silu(causal_depthwise_conv(xbc, conv_w, conv_b))
    xs, bm, cm = jnp.split(xbc, [D_INNER, D_INNER + SSM_GROUPS * SSM_STATE], axis=-1)
    xs = xs.reshape(bsz, seq, SSM_HEADS, SSM_HEAD_DIM)
    bm = bm.reshape(bsz, seq, SSM_GROUPS, SSM_STATE)
    cm = cm.reshape(bsz, seq, SSM_GROUPS, SSM_STATE)
    dt = jax.nn.softplus(dt_raw.astype(jnp.float32) + dt_bias.astype(jnp.float32))
    a_head = -jnp.exp(a_log.astype(jnp.float32))
    y = ssd_chunked(xs, dt, a_head, bm, cm) + d_skip.astype(jnp.float32)[:, None] * xs.astype(jnp.float32)
    yg = (y.reshape(bsz, seq, D_INNER) * jax.nn.silu(z.astype(jnp.float32)))
    yg = yg.reshape(bsz, seq, SSM_GROUPS, D_INNER // SSM_GROUPS)
    yg = yg * lax.rsqrt(jnp.mean(yg * yg, axis=-1, keepdims=True) + EPS)
    return (yg.reshape(bsz, seq, D_INNER) * norm_w.astype(jnp.float32)).astype(z.dtype)


def dsa_branch(q, k, v, q_idx, k_idx, w_idx):
    bsz, seq = q.shape[:2]
    top_k = min(TOPK_MAX, seq // 4)
    nb = seq // Q_BLOCK
    rep = ATT_HEADS // ATT_KV_HEADS
    q = q.reshape(bsz, seq, ATT_KV_HEADS, rep, ATT_HEAD_DIM)
    k = k.reshape(bsz, seq, ATT_KV_HEADS, ATT_HEAD_DIM)
    v = v.reshape(bsz, seq, ATT_KV_HEADS, ATT_HEAD_DIM)
    q_idx = q_idx.reshape(bsz, seq, IDX_HEADS, IDX_DIM)
    k_idx_f = k_idx.astype(jnp.float32)
    slopes = (2.0 ** (-ALIBI_BASE * jnp.arange(1, ATT_HEADS + 1, dtype=jnp.float32) / ATT_HEADS)
              ).reshape(ATT_KV_HEADS, rep)
    key_pos = jnp.arange(seq, dtype=jnp.int32)
    gather_rows = jax.vmap(lambda kk, ii: kk[ii])

    def to_blocks(a):
        return jnp.moveaxis(a.reshape((bsz, nb, Q_BLOCK) + a.shape[2:]), 1, 0)

    def attend(blk):
        qb, qib, wb, t0 = blk
        tq = t0 + jnp.arange(Q_BLOCK, dtype=jnp.int32)
        admissible = key_pos[None, :] <= tq[:, None]
        rel = jax.nn.relu(jnp.einsum("bqjd,bsd->bqjs", qib.astype(jnp.float32), k_idx_f)
                          * (IDX_DIM ** -0.5))
        score = jnp.einsum("bqj,bqjs->bqs", wb.astype(jnp.float32) * (IDX_HEADS ** -0.5), rel)
        score = jnp.where(admissible[None], score, -jnp.inf)
        _, sel = lax.top_k(score, top_k)
        k_sel = gather_rows(k, sel)
        v_sel = gather_rows(v, sel)
        logits = jnp.einsum("bqgrd,bqkgd->bgrqk", qb, k_sel).astype(jnp.float32) * (ATT_HEAD_DIM ** -0.5)
        dist = jnp.abs(tq[None, :, None] - sel).astype(jnp.float32)
        logits = logits - slopes[None, :, :, None, None] * dist[:, None, None]
        valid = sel <= tq[None, :, None]
        logits = jnp.where(valid[:, None, None], logits, -jnp.inf)
        p = jax.nn.softmax(logits, axis=-1).astype(v.dtype)
        o = jnp.einsum("bgrqk,bqkgd->bqgrd", p, v_sel)
        return o.reshape(bsz, Q_BLOCK, ATT_HEADS * ATT_HEAD_DIM)

    out = lax.map(attend, (to_blocks(q), to_blocks(q_idx), to_blocks(w_idx),
                           jnp.arange(nb, dtype=jnp.int32) * Q_BLOCK))
    return jnp.moveaxis(out, 0, 1).reshape(bsz, seq, ATT_HEADS * ATT_HEAD_DIM)


def hybrid_mixer(h, w_in, conv_w, conv_b, dt_bias, a_log, d_skip, ssm_norm,
                 w_branch_ssm, w_branch_attn, w_out):
    proj = h @ w_in
    split_points = np.cumsum(IN_SIZES)[:-1].tolist()
    z, xbc, dt_raw, q, k, v, q_idx, k_idx, w_idx, g_ssm, g_att = jnp.split(proj, split_points, axis=-1)
    y_ssm = mamba2_branch(z, xbc, dt_raw, conv_w, conv_b, dt_bias, a_log, d_skip, ssm_norm)
    y_att = dsa_branch(q, k, v, q_idx, k_idx, w_idx)
    merged = jax.nn.sigmoid(g_ssm) * (y_ssm @ w_branch_ssm) + jax.nn.sigmoid(g_att) * (y_att @ w_branch_attn)
    return merged @ w_out


def setup_inputs(seed: int = 0) -> dict:
    key = jax.random.key(seed)
    ks = jax.random.split(key, 24)
    f32 = jnp.float32

    def nrm(k_, shape, scale):
        return jax.random.normal(k_, shape, f32) * scale

    def gain(k_, width):
        return 1.0 + 0.02 * jax.random.normal(k_, (DEPTH, width), f32)

    dt0 = jnp.exp(jax.random.uniform(ks[10], (DEPTH, SSM_HEADS), f32)
                  * (np.log(DT_MAX) - np.log(DT_MIN)) + np.log(DT_MIN))
    dt_bias = dt0 + jnp.log(-jnp.expm1(-dt0))
    return {
        "x": jax.random.normal(ks[0], (BATCH, SEQ, D_MODEL), f32),
        "ffn1_norm": gain(ks[1], D_MODEL),
        "ffn1_w_gate": nrm(ks[2], (DEPTH, D_MODEL, D_FF), D_MODEL ** -0.5),
        "ffn1_w_up": nrm(ks[3], (DEPTH, D_MODEL, D_FF), D_MODEL ** -0.5),
        "ffn1_w_down": nrm(ks[4], (DEPTH, D_FF, D_MODEL), D_FF ** -0.5),
        "mix_norm": gain(ks[5], D_MODEL),
        "w_in": nrm(ks[6], (DEPTH, D_MODEL, IN_W), D_MODEL ** -0.5),
        "conv_w": nrm(ks[7], (DEPTH, CONV_K, XBC_W), CONV_K ** -0.5),
        "conv_b": nrm(ks[8], (DEPTH, XBC_W), 0.02),
        "dt_bias": dt_bias,
        "a_log": jnp.log(jax.random.uniform(ks[11], (DEPTH, SSM_HEADS), f32, 1.0, 16.0)),
        "d_skip": 1.0 + 0.1 * jax.random.normal(ks[12], (DEPTH, SSM_HEADS), f32),
        "ssm_norm": gain(ks[13], D_INNER),
        "w_branch_ssm": nrm(ks[14], (DEPTH, D_INNER, D_MODEL), D_INNER ** -0.5),
        "w_branch_attn": nrm(ks[15], (DEPTH, ATT_HEADS * ATT_HEAD_DIM, D_MODEL), (ATT_HEADS * ATT_HEAD_DIM) ** -0.5),
        "w_out": nrm(ks[16], (DEPTH, D_MODEL, D_MODEL), D_MODEL ** -0.5),
        "ffn2_norm": gain(ks[17], D_MODEL),
        "ffn2_w_gate": nrm(ks[18], (DEPTH, D_MODEL, D_FF), D_MODEL ** -0.5),
        "ffn2_w_up": nrm(ks[19], (DEPTH, D_MODEL, D_FF), D_MODEL ** -0.5),
        "ffn2_w_down": nrm(ks[20], (DEPTH, D_FF, D_MODEL), D_FF ** -0.5),
        "final_norm": 1.0 + 0.02 * jax.random.normal(ks[21], (D_MODEL,), f32),
    }


def reference(x, ffn1_norm, ffn1_w_gate, ffn1_w_up, ffn1_w_down, mix_norm, w_in, conv_w, conv_b,
              dt_bias, a_log, d_skip, ssm_norm, w_branch_ssm, w_branch_attn, w_out,
              ffn2_norm, ffn2_w_gate, ffn2_w_up, ffn2_w_down, final_norm):
    for l in range(DEPTH):
        x = x + FFN_RESIDUAL_SCALE * swiglu(rms_norm(x, ffn1_norm[l]), ffn1_w_gate[l], ffn1_w_up[l], ffn1_w_down[l])
        x = x + hybrid_mixer(rms_norm(x, mix_norm[l]), w_in[l], conv_w[l], conv_b[l], dt_bias[l], a_log[l],
                             d_skip[l], ssm_norm[l], w_branch_ssm[l], w_branch_attn[l], w_out[l])
        x = x + FFN_RESIDUAL_SCALE * swiglu(rms_norm(x, ffn2_norm[l]), ffn2_w_gate[l], ffn2_w_up[l], ffn2_w_down[l])
    return rms_norm(x, final_norm)
```

```python
import functools

import jax
import jax.numpy as jnp
from jax import lax
from jax.experimental import pallas as pl
from jax.experimental.pallas import tpu as pltpu

F32 = jnp.float32
BF16 = jnp.bfloat16

D_MODEL = 1024
D_INNER = 2048
SSM_HEAD_DIM = 64
SSM_HEADS = 32
SSM_GROUPS = 4
SSM_STATE = 128
CONV_K = 4
XBC_W = D_INNER + 2 * SSM_GROUPS * SSM_STATE
ATT_HEADS = 16
ATT_HEAD_DIM = 64
ATT_KV_HEADS = 4
IDX_HEADS = 8
IDX_DIM = 64
TOPK_MAX = 256
ALIBI_BASE = 8.0
D_FF = 2816
FFN_RESIDUAL_SCALE = 0.5
EPS = 1e-6

IN_SIZES = (D_INNER, XBC_W, SSM_HEADS, ATT_HEADS * ATT_HEAD_DIM, ATT_KV_HEADS * ATT_HEAD_DIM,
            ATT_KV_HEADS * ATT_HEAD_DIM, IDX_HEADS * IDX_DIM, IDX_DIM, IDX_HEADS, D_MODEL, D_MODEL)

VMEM_LIMIT_BYTES = 56 * 1024 * 1024
SSD_CHUNK = 128
DSA_QBLOCK = 128
FFN_ROWS = 512
FFN_FCHUNK = 256
PROJ_ROWS = 256
MERGE_ROWS = 512
MASK_NEG = -1e30


def _rms(x, g):
    ms = jnp.mean(x * x, axis=-1, keepdims=True)
    return x * lax.rsqrt(ms + EPS) * g


def _const_spec(shape):
    return pl.BlockSpec(shape, lambda *_: (0,) * len(shape), pipeline_mode=pl.Buffered(1))


def _params(*sem):
    return pltpu.CompilerParams(dimension_semantics=sem, vmem_limit_bytes=VMEM_LIMIT_BYTES)


def _ffn_kernel(x_ref, g_ref, wg_ref, wu_ref, wd_ref, *rest, final_norm):
    if final_norm:
        fg_ref, o_ref, acc_ref = rest
    else:
        o_ref, acc_ref = rest
    x = x_ref[...]
    xn = _rms(x, g_ref[...]).astype(BF16)
    d_ff = wg_ref.shape[1]
    for c in range(0, d_ff, FFN_FCHUNK):
        gate = jnp.dot(xn, wg_ref[:, c:c + FFN_FCHUNK], preferred_element_type=F32)
        up = jnp.dot(xn, wu_ref[:, c:c + FFN_FCHUNK], preferred_element_type=F32)
        act = (gate * jax.nn.sigmoid(gate) * up).astype(BF16)
        part = jnp.dot(act, wd_ref[c:c + FFN_FCHUNK, :], preferred_element_type=F32)
        if c == 0:
            acc_ref[...] = part
        else:
            acc_ref[...] += part
    y = x + FFN_RESIDUAL_SCALE * acc_ref[...]
    if final_norm:
        y = _rms(y, fg_ref[...])
    o_ref[...] = y


def _ffn(x, norm_g, wg, wu, wd, final_g=None):
    t, d = x.shape
    d_ff = wg.shape[1]
    rows = min(FFN_ROWS, t)
    in_specs = [pl.BlockSpec((rows, d), lambda i: (i, 0)), _const_spec((1, d)),
                _const_spec((d, d_ff)), _const_spec((d, d_ff)), _const_spec((d_ff, d))]
    args = [x, norm_g.reshape(1, d), wg, wu, wd]
    if final_g is not None:
        in_specs.append(_const_spec((1, d)))
        args.append(final_g.reshape(1, d))
    return pl.pallas_call(
        functools.partial(_ffn_kernel, final_norm=final_g is not None),
        out_shape=jax.ShapeDtypeStruct((t, d), F32),
        grid=(t // rows,),
        in_specs=in_specs,
        out_specs=pl.BlockSpec((rows, d), lambda i: (i, 0)),
        scratch_shapes=[pltpu.VMEM((rows, d), F32)],
        compiler_params=_params("parallel"),
        name="ffn_final" if final_g is not None else "ffn",
    )(*args)


def _in_proj_kernel(x_ref, g_ref, wz, wxbc, wq, wk, wv, wqi, wgs, wga, wmisc,
                    oz, oxbc, oq, ok, ov, oqi, ogs, oga, odt, oki, owi):
    xn = _rms(x_ref[...], g_ref[...]).astype(BF16)
    for w_ref, o_ref in ((wz, oz), (wxbc, oxbc), (wq, oq), (wk, ok), (wv, ov), (wqi, oqi), (wgs, ogs), (wga, oga)):
        n = w_ref.shape[1]
        step = min(n, 512)
        for c in range(0, n, step):
            o_ref[:, c:c + step] = jnp.dot(xn, w_ref[:, c:c + step], preferred_element_type=F32).astype(o_ref.dtype)
    misc = jnp.dot(xn, wmisc[...], preferred_element_type=F32)
    odt[...] = misc[:, :SSM_HEADS]
    oki[...] = misc[:, SSM_HEADS:SSM_HEADS + IDX_DIM].astype(BF16)
    owi[...] = misc[:, SSM_HEADS + IDX_DIM:SSM_HEADS + IDX_DIM + IDX_HEADS]


def _in_proj(x1, norm_g, w_in):
    t, d = x1.shape
    offs = [0]
    for s in IN_SIZES:
        offs.append(offs[-1] + s)
    seg = lambda i: w_in[:, offs[i]:offs[i + 1]]
    w_misc = jnp.concatenate(
        [seg(2), seg(7), seg(8), jnp.zeros((d, 128 - SSM_HEADS - IDX_DIM - IDX_HEADS), w_in.dtype)], axis=1)
    big = [seg(0), seg(1), seg(3), seg(4), seg(5), seg(6), seg(9), seg(10)]
    weights = [w.astype(BF16) for w in big + [w_misc]]
    rows = min(PROJ_ROWS, t)
    row_spec = lambda n: pl.BlockSpec((rows, n), lambda i: (i, 0))
    out_shapes = [jax.ShapeDtypeStruct((t, w.shape[1]), BF16) for w in big]
    out_shapes += [jax.ShapeDtypeStruct((t, SSM_HEADS), F32), jax.ShapeDtypeStruct((t, IDX_DIM), BF16),
                   jax.ShapeDtypeStruct((t, IDX_HEADS), F32)]
    return pl.pallas_call(
        _in_proj_kernel,
        out_shape=out_shapes,
        grid=(t // rows,),
        in_specs=[row_spec(d), _const_spec((1, d))] + [_const_spec(w.shape) for w in weights],
        out_specs=[row_spec(s.shape[1]) for s in out_shapes],
        compiler_params=_params("parallel"),
        name="in_proj",
    )(x1, norm_g.reshape(1, d), *weights)


def _split_dot(a, b01, parts):
    out = None
    rem = a
    for _ in range(parts):
        hi = rem.astype(BF16)
        term = jnp.dot(hi, b01, preferred_element_type=F32)
        out = term if out is None else out + term
        rem = rem - hi.astype(F32)
    return out


def _ssd_kernel(xbc_ref, z_ref, dt_ref, convw_ref, convb_ref, dtb_ref, ahead_ref, dexp_ref, normw_ref,
                o_ref, win_ref, state_ref):
    q = SSD_CHUNK
    hd, n, g = SSM_HEAD_DIM, SSM_STATE, SSM_GROUPS
    heads = SSM_HEADS
    hpg = heads // g
    c = pl.program_id(1)

    @pl.when(c == 0)
    def _():
        win_ref[0:8, :] = jnp.zeros((8, XBC_W), F32)
        state_ref[...] = jnp.zeros_like(state_ref)

    win_ref[8:8 + q, :] = xbc_ref[...].astype(F32)
    conv = convb_ref[...]
    for k in range(CONV_K):
        conv = conv + convw_ref[k:k + 1, :] * win_ref[8 - (CONV_K - 1) + k:8 - (CONV_K - 1) + k + q, :]
    win_ref[0:8, :] = win_ref[q:q + 8, :]
    xbc = conv * jax.nn.sigmoid(conv)
    xs = xbc[:, :D_INNER]
    bm = xbc[:, D_INNER:D_INNER + g * n]
    cm = xbc[:, D_INNER + g * n:]

    dt = jax.nn.softplus(dt_ref[...] + dtb_ref[...])
    a = dt * ahead_ref[...]
    row = lax.broadcasted_iota(jnp.int32, (q, q), 0)
    col = lax.broadcasted_iota(jnp.int32, (q, q), 1)
    causal = row >= col
    tril = jnp.where(causal, 1.0, 0.0).astype(BF16)
    a_cs = _split_dot_left(tril, a)
    a_cs_t = _transpose_heads(a_cs)
    dt_t = _transpose_heads(dt)
    a_last = a_cs[q - 1:q, :]
    decay_to_end = jnp.exp(a_last - a_cs)
    lane = lax.broadcasted_iota(jnp.int32, (heads, heads * hd), 1)
    first = lax.broadcasted_iota(jnp.int32, (heads, heads * hd), 0) * hd
    expand = jnp.where((lane >= first) & (lane < first + hd), 1.0, 0.0).astype(BF16)
    w_state = _split_dot(dt * decay_to_end, expand, 2)
    chunk_decay = _split_dot(jnp.broadcast_to(jnp.exp(a_last), (8, heads)), expand, 2)[0:1, :]

    prev = state_ref[...]
    prev_b = prev.astype(BF16)
    xs_b = xs.astype(BF16)
    ys = []
    for gi in range(g):
        bg = bm[:, gi * n:(gi + 1) * n].astype(BF16)
        cg = cm[:, gi * n:(gi + 1) * n]
        cb = lax.dot_general(cg.astype(BF16), bg, (((1,), (1,)), ((), ())), preferred_element_type=F32)
        for r in range(hpg):
            h = gi * hpg + r
            a_col = jnp.broadcast_to(a_cs[:, h:h + 1], (q, q))
            seg = a_col - a_cs_t[h:h + 1, :]
            decay = jnp.where(causal, jnp.exp(jnp.where(causal, seg, 0.0)), 0.0)
            m_diag = (cb * decay * dt_t[h:h + 1, :]).astype(BF16)
            m_off = (cg * jnp.exp(a_col)).astype(BF16)
            lhs = jnp.concatenate([m_diag, m_off], axis=1)
            rhs = jnp.concatenate([xs_b[:, h * hd:(h + 1) * hd], prev_b[:, h * hd:(h + 1) * hd]], axis=0)
            ys.append(jnp.dot(lhs, rhs, preferred_element_type=F32))
        xw = (xs[:, gi * hpg * hd:(gi + 1) * hpg * hd] * w_state[:, gi * hpg * hd:(gi + 1) * hpg * hd]).astype(BF16)
        s_new = lax.dot_general(bg, xw, (((0,), (0,)), ((), ())), preferred_element_type=F32)
        sl = slice(gi * hpg * hd, (gi + 1) * hpg * hd)
        state_ref[:, sl] = prev[:, sl] * chunk_decay[:, sl] + s_new
    y = jnp.concatenate(ys, axis=1) + dexp_ref[...] * xs

    z = z_ref[...].astype(F32)
    yg = y * (z * jax.nn.sigmoid(z))
    gw = D_INNER // g
    outs = []
    for gi in range(g):
        blk = yg[:, gi * gw:(gi + 1) * gw]
        ms = jnp.mean(blk * blk, axis=-1, keepdims=True)
        outs.append(blk * lax.rsqrt(ms + EPS))
    o_ref[...] = (jnp.concatenate(outs, axis=1) * normw_ref[...]).astype(o_ref.dtype)


def _transpose_heads(x):
    q, h = x.shape
    return jnp.concatenate([x, jnp.zeros((q, q - h), x.dtype)], axis=1).T[:h, :]


def _split_dot_left(b01, a):
    out = None
    rem = a
    for _ in range(3):
        hi = rem.astype(BF16)
        term = jnp.dot(b01, hi, preferred_element_type=F32)
        out = term if out is None else out + term
        rem = rem - hi.astype(F32)
    return out


def _ssd(xbc, z, dt_raw, conv_w, conv_b, dt_bias, a_log, d_skip, norm_w, bsz, seq):
    q = SSD_CHUNK
    nc = seq // q
    a_head = -jnp.exp(a_log.astype(F32)).reshape(1, SSM_HEADS)
    d_exp = jnp.repeat(d_skip.astype(F32), SSM_HEAD_DIM).reshape(1, D_INNER)
    row_spec = lambda n: pl.BlockSpec((q, n), lambda b, c: (b * nc + c, 0))
    return pl.pallas_call(
        _ssd_kernel,
        out_shape=jax.ShapeDtypeStruct((bsz * seq, D_INNER), BF16),
        grid=(bsz, nc),
        in_specs=[row_spec(XBC_W), row_spec(D_INNER), row_spec(SSM_HEADS),
                  _const_spec((CONV_K, XBC_W)), _const_spec((1, XBC_W)), _const_spec((1, SSM_HEADS)),
                  _const_spec((1, SSM_HEADS)), _const_spec((1, D_INNER)), _const_spec((1, D_INNER))],
        out_specs=row_spec(D_INNER),
        scratch_shapes=[pltpu.VMEM((q + 8, XBC_W), F32), pltpu.VMEM((SSM_STATE, D_INNER), F32)],
        compiler_params=_params("parallel", "arbitrary"),
        name="ssd",
    )(xbc, z, dt_raw, conv_w.astype(F32), conv_b.reshape(1, XBC_W).astype(F32),
      dt_bias.reshape(1, SSM_HEADS).astype(F32), a_head, d_exp, norm_w.reshape(1, D_INNER).astype(F32))


def _sortable_to_f32(s):
    bits = jnp.where(s < 0, s ^ jnp.int32(-2 ** 31), ~s)
    return lax.bitcast_convert_type(bits, F32)


def _count(mask):
    return jnp.sum(jnp.where(mask, 1.0, 0.0), axis=1, keepdims=True)


def _dsa_kernel(q_ref, k_ref, v_ref, qi_ref, ki_ref, wi_ref, o_ref, *, top_k):
    tq = DSA_QBLOCK
    seq = k_ref.shape[0]
    i = pl.program_id(1)
    t0 = i * tq
    s_idx = lax.broadcasted_iota(jnp.int32, (tq, seq), 1)
    t_idx = t0 + lax.broadcasted_iota(jnp.int32, (tq, seq), 0)
    adm = s_idx <= t_idx

    ki = ki_ref[...]
    w = wi_ref[...] * (IDX_HEADS ** -0.5)
    score = jnp.zeros((tq, seq), F32)
    for j in range(IDX_HEADS):
        qj = qi_ref[:, j * IDX_DIM:(j + 1) * IDX_DIM]
        rel = lax.dot_general(qj, ki, (((1,), (1,)), ((), ())), preferred_element_type=F32)
        score = score + w[:, j:j + 1] * jnp.maximum(rel * (IDX_DIM ** -0.5), 0.0)
    score = jnp.where(adm, score, -jnp.inf)

    kf = float(top_k)

    def value_step(b, prefix):
        cand = prefix | lax.shift_left(jnp.int32(1), 31 - b)
        cnt = _count(score >= _sortable_to_f32(cand))
        return jnp.where(cnt >= kf, cand, prefix)

    prefix = lax.fori_loop(0, 32, value_step, jnp.zeros((tq, 1), jnp.int32))
    thr = _sortable_to_f32(prefix)
    above = score > thr
    tie = score == thr
    need = kf - _count(above)

    def index_step(b, lo):
        cand = lo + lax.shift_left(jnp.int32(1), b)
        cnt = _count(tie & (s_idx < cand))
        return jnp.where(cnt < need, cand, lo)

    nbits = (seq - 1).bit_length()
    last = lax.fori_loop(0, nbits, lambda b, lo: index_step(nbits - 1 - b, lo), jnp.zeros((tq, 1), jnp.int32))
    sel = above | (tie & (s_idx <= last))
    early = t_idx < top_k
    sel = (early & adm) | (~early & sel)

    neg_dist = jnp.where(sel, (s_idx - t_idx).astype(F32), MASK_NEG)
    rep = ATT_HEADS // ATT_KV_HEADS
    outs = []
    for g in range(ATT_KV_HEADS):
        kg = k_ref[:, g * ATT_HEAD_DIM:(g + 1) * ATT_HEAD_DIM]
        vg = v_ref[:, g * ATT_HEAD_DIM:(g + 1) * ATT_HEAD_DIM]
        for r in range(rep):
            h = g * rep + r
            slope = 2.0 ** (-ALIBI_BASE * (h + 1) / ATT_HEADS)
            qh = q_ref[:, h * ATT_HEAD_DIM:(h + 1) * ATT_HEAD_DIM] * (ATT_HEAD_DIM ** -0.5)
            logits = lax.dot_general(qh, kg, (((1,), (1,)), ((), ())), preferred_element_type=F32)
            logits = logits + slope * neg_dist
            m = jnp.max(logits, axis=1, keepdims=True)
            p = jnp.exp(logits - m)
            denom = jnp.sum(p, axis=1, keepdims=True)
            o = jnp.dot(p.astype(BF16), vg, preferred_element_type=F32)
            outs.append(o / denom)
    o_ref[...] = jnp.concatenate(outs, axis=1).astype(o_ref.dtype)


def _dsa(q, k, v, q_idx, k_idx, w_idx, bsz, seq):
    tq = DSA_QBLOCK
    nq = seq // tq
    top_k = min(TOPK_MAX, seq // 4)
    blk = lambda n: pl.BlockSpec((tq, n), lambda b, i: (b * nq + i, 0))
    full = lambda n: pl.BlockSpec((seq, n), lambda b, i: (b, 0))
    return pl.pallas_call(
        functools.partial(_dsa_kernel, top_k=top_k),
        out_shape=jax.ShapeDtypeStruct((bsz * seq, ATT_HEADS * ATT_HEAD_DIM), BF16),
        grid=(bsz, nq),
        in_specs=[blk(q.shape[1]), full(k.shape[1]), full(v.shape[1]), blk(q_idx.shape[1]),
                  full(k_idx.shape[1]), blk(w_idx.shape[1])],
        out_specs=blk(ATT_HEADS * ATT_HEAD_DIM),
        compiler_params=_params("parallel", "arbitrary"),
        name="dsa",
    )(q, k, v, q_idx, k_idx, w_idx)


def _merge_kernel(x_ref, ya_ref, yb_ref, ga_ref, gb_ref, wa_ref, wb_ref, wo_ref, o_ref):
    pa = jnp.dot(ya_ref[...], wa_ref[...], preferred_element_type=F32)
    pb = jnp.dot(yb_ref[...], wb_ref[...], preferred_element_type=F32)
    merged = (jax.nn.sigmoid(ga_ref[...].astype(F32)) * pa + jax.nn.sigmoid(gb_ref[...].astype(F32)) * pb)
    o_ref[...] = x_ref[...] + jnp.dot(merged.astype(BF16), wo_ref[...], preferred_element_type=F32)


def _merge(x1, y_ssm, y_att, g_ssm, g_att, w_a, w_b, w_o):
    t, d = x1.shape
    rows = min(MERGE_ROWS, t)
    row_spec = lambda n: pl.BlockSpec((rows, n), lambda i: (i, 0))
    return pl.pallas_call(
        _merge_kernel,
        out_shape=jax.ShapeDtypeStruct((t, d), F32),
        grid=(t // rows,),
        in_specs=[row_spec(d), row_spec(y_ssm.shape[1]), row_spec(y_att.shape[1]), row_spec(d), row_spec(d),
                  _const_spec(w_a.shape), _const_spec(w_b.shape), _const_spec(w_o.shape)],
        out_specs=row_spec(d),
        compiler_params=_params("parallel"),
        name="merge",
    )(x1, y_ssm, y_att, g_ssm, g_att, w_a, w_b, w_o)


def kernel(x, ffn1_norm, ffn1_w_gate, ffn1_w_up, ffn1_w_down, mix_norm, w_in, conv_w, conv_b, dt_bias, a_log,
           d_skip, ssm_norm, w_branch_ssm, w_branch_attn, w_out, ffn2_norm, ffn2_w_gate, ffn2_w_up, ffn2_w_down,
           final_norm):
    bsz, seq, d = x.shape
    depth = ffn1_norm.shape[0]
    h = x.reshape(bsz * seq, d)
    for l in range(depth):
        last = l == depth - 1
        h = _ffn(h, ffn1_norm[l], ffn1_w_gate[l].astype(BF16), ffn1_w_up[l].astype(BF16),
                 ffn1_w_down[l].astype(BF16))
        z, xbc, q, k, v, q_idx, g_ssm, g_att, dt_raw, k_idx, w_idx = _in_proj(h, mix_norm[l], w_in[l])
        y_ssm = _ssd(xbc, z, dt_raw, conv_w[l], conv_b[l], dt_bias[l], a_log[l], d_skip[l], ssm_norm[l], bsz, seq)
        y_att = _dsa(q, k, v, q_idx, k_idx, w_idx, bsz, seq)
        h = _merge(h, y_ssm, y_att, g_ssm, g_att, w_branch_ssm[l].astype(BF16), w_branch_attn[l].astype(BF16),
                   w_out[l].astype(BF16))
        h = _ffn(h, ffn2_norm[l], ffn2_w_gate[l].astype(BF16), ffn2_w_up[l].astype(BF16),
                 ffn2_w_down[l].astype(BF16), final_g=final_norm if last else None)
    return h.reshape(bsz, seq, d)
```

```python
import functools

import jax
import jax.numpy as jnp
from jax import lax
from jax.experimental import pallas as pl
from jax.experimental.pallas import tpu as pltpu

F32 = jnp.float32
BF16 = jnp.bfloat16

D_MODEL = 1024
D_INNER = 2048
SSM_HEAD_DIM = 64
SSM_HEADS = 32
SSM_GROUPS = 4
SSM_STATE = 128
CONV_K = 4
XBC_W = D_INNER + 2 * SSM_GROUPS * SSM_STATE
ATT_HEADS = 16
ATT_HEAD_DIM = 64
ATT_KV_HEADS = 4
IDX_HEADS = 8
IDX_DIM = 64
TOPK_MAX = 256
ALIBI_BASE = 8.0
D_FF = 2816
FFN_RESIDUAL_SCALE = 0.5
EPS = 1e-6

IN_SIZES = (D_INNER, XBC_W, SSM_HEADS, ATT_HEADS * ATT_HEAD_DIM, ATT_KV_HEADS * ATT_HEAD_DIM,
            ATT_KV_HEADS * ATT_HEAD_DIM, IDX_HEADS * IDX_DIM, IDX_DIM, IDX_HEADS, D_MODEL, D_MODEL)

VMEM_LIMIT_BYTES = 56 * 1024 * 1024
SSD_CHUNK = 128
DSA_QBLOCK = 128
FFN_ROWS = 512
FFN_FCHUNK = 256
PROJ_ROWS = 256
MERGE_ROWS = 512
MASK_NEG = -1e30


def _rms(x, g):
    ms = jnp.mean(x * x, axis=-1, keepdims=True)
    return x * lax.rsqrt(ms + EPS) * g


def _const_spec(shape):
    return pl.BlockSpec(shape, lambda *_: (0,) * len(shape), pipeline_mode=pl.Buffered(1))


def _params(*sem):
    return pltpu.CompilerParams(dimension_semantics=sem, vmem_limit_bytes=VMEM_LIMIT_BYTES)


def _ffn_kernel(x_ref, g_ref, wg_ref, wu_ref, wd_ref, *rest, final_norm):
    if final_norm:
        fg_ref, o_ref, acc_ref = rest
    else:
        o_ref, acc_ref = rest
    x = x_ref[...]
    xn = _rms(x, g_ref[...]).astype(BF16)
    d_ff = wg_ref.shape[1]
    for c in range(0, d_ff, FFN_FCHUNK):
        gate = jnp.dot(xn, wg_ref[:, c:c + FFN_FCHUNK], preferred_element_type=F32)
        up = jnp.dot(xn, wu_ref[:, c:c + FFN_FCHUNK], preferred_element_type=F32)
        act = (gate * jax.nn.sigmoid(gate) * up).astype(BF16)
        part = jnp.dot(act, wd_ref[c:c + FFN_FCHUNK, :], preferred_element_type=F32)
        if c == 0:
            acc_ref[...] = part
        else:
            acc_ref[...] += part
    y = x + FFN_RESIDUAL_SCALE * acc_ref[...]
    if final_norm:
        y = _rms(y, fg_ref[...])
    o_ref[...] = y


def _ffn(x, norm_g, wg, wu, wd, final_g=None):
    t, d = x.shape
    d_ff = wg.shape[1]
    rows = min(FFN_ROWS, t)
    in_specs = [pl.BlockSpec((rows, d), lambda i: (i, 0)), _const_spec((1, d)),
                _const_spec((d, d_ff)), _const_spec((d, d_ff)), _const_spec((d_ff, d))]
    args = [x, norm_g.reshape(1, d), wg, wu, wd]
    if final_g is not None:
        in_specs.append(_const_spec((1, d)))
        args.append(final_g.reshape(1, d))
    return pl.pallas_call(
        functools.partial(_ffn_kernel, final_norm=final_g is not None),
        out_shape=jax.ShapeDtypeStruct((t, d), F32),
        grid=(t // rows,),
        in_specs=in_specs,
        out_specs=pl.BlockSpec((rows, d), lambda i: (i, 0)),
        scratch_shapes=[pltpu.VMEM((rows, d), F32)],
        compiler_params=_params("parallel"),
        name="ffn_final" if final_g is not None else "ffn",
    )(*args)


def _dot_nt(a, b):
    return lax.dot_general(a, b, (((1,), (1,)), ((), ())), preferred_element_type=F32)


def _in_proj_kernel(x_ref, g_ref, wz, wxbc, wk, wgs, wga, wmisc, wqt, wqit, wvt, wwit,
                    oz, oxbc, ok, ogs, oga, odt, oki, oqt, oqit, ovt, owit):
    xn = _rms(x_ref[...], g_ref[...]).astype(BF16)
    for w_ref, o_ref in ((wz, oz), (wxbc, oxbc), (wk, ok), (wgs, ogs), (wga, oga)):
        n = w_ref.shape[1]
        step = min(n, 512)
        for c in range(0, n, step):
            o_ref[:, c:c + step] = jnp.dot(xn, w_ref[:, c:c + step], preferred_element_type=F32).astype(o_ref.dtype)
    misc = jnp.dot(xn, wmisc[...], preferred_element_type=F32)
    odt[...] = misc[:, :SSM_HEADS]
    oki[...] = misc[:, SSM_HEADS:SSM_HEADS + IDX_DIM].astype(BF16)
    for w_ref, o_ref in ((wqt, oqt), (wqit, oqit)):
        n = w_ref.shape[0]
        for c in range(0, n, 256):
            o_ref[c:c + 256, :] = _dot_nt(w_ref[c:c + 256, :], xn).astype(o_ref.dtype)
    owit[...] = _dot_nt(wwit[...], xn)[:IDX_HEADS, :]
    vt = _dot_nt(wvt[...], xn).astype(BF16)
    for j in range(ovt.shape[0]):
        ovt[j] = vt[:, j * DSA_QBLOCK:(j + 1) * DSA_QBLOCK]


def _in_proj(x1, norm_g, w_in):
    t, d = x1.shape
    offs = [0]
    for s in IN_SIZES:
        offs.append(offs[-1] + s)
    seg = lambda i: w_in[:, offs[i]:offs[i + 1]]
    w_misc = jnp.concatenate([seg(2), seg(7), jnp.zeros((d, 128 - SSM_HEADS - IDX_DIM), w_in.dtype)], axis=1)
    natural = [seg(0), seg(1), seg(4), seg(9), seg(10), w_misc]
    w_idx_t = jnp.concatenate([seg(8).T, jnp.zeros((16 - IDX_HEADS, d), w_in.dtype)], axis=0)
    transposed = [seg(3).T, seg(6).T, seg(5).T, w_idx_t]
    weights = [w.astype(BF16) for w in natural + transposed]
    rows = min(PROJ_ROWS, t)
    nchunk = rows // DSA_QBLOCK
    row_spec = lambda n: pl.BlockSpec((rows, n), lambda i: (i, 0))
    col_spec = lambda n: pl.BlockSpec((n, rows), lambda i: (0, i))
    kv_w = ATT_KV_HEADS * ATT_HEAD_DIM
    out_shapes = [jax.ShapeDtypeStruct((t, w.shape[1]), BF16) for w in natural[:-1]]
    out_shapes += [jax.ShapeDtypeStruct((t, SSM_HEADS), F32), jax.ShapeDtypeStruct((t, IDX_DIM), BF16),
                   jax.ShapeDtypeStruct((ATT_HEADS * ATT_HEAD_DIM, t), BF16),
                   jax.ShapeDtypeStruct((IDX_HEADS * IDX_DIM, t), BF16),
                   jax.ShapeDtypeStruct((t // DSA_QBLOCK, kv_w, DSA_QBLOCK), BF16),
                   jax.ShapeDtypeStruct((IDX_HEADS, t), F32)]
    out_specs = [row_spec(s.shape[1]) for s in out_shapes[:7]]
    out_specs += [col_spec(ATT_HEADS * ATT_HEAD_DIM), col_spec(IDX_HEADS * IDX_DIM),
                  pl.BlockSpec((nchunk, kv_w, DSA_QBLOCK), lambda i: (i, 0, 0)), col_spec(IDX_HEADS)]
    return pl.pallas_call(
        _in_proj_kernel,
        out_shape=out_shapes,
        grid=(t // rows,),
        in_specs=[row_spec(d), _const_spec((1, d))] + [_const_spec(w.shape) for w in weights],
        out_specs=out_specs,
        compiler_params=_params("parallel"),
        name="in_proj",
    )(x1, norm_g.reshape(1, d), *weights)


def _split_dot(a, b01, parts):
    out = None
    rem = a
    for _ in range(parts):
        hi = rem.astype(BF16)
        term = jnp.dot(hi, b01, preferred_element_type=F32)
        out = term if out is None else out + term
        rem = rem - hi.astype(F32)
    return out


def _ssd_kernel(xbc_ref, z_ref, dt_ref, convw_ref, convb_ref, dtb_ref, ahead_ref, dexp_ref, normw_ref,
                o_ref, win_ref, state_ref):
    q = SSD_CHUNK
    hd, n, g = SSM_HEAD_DIM, SSM_STATE, SSM_GROUPS
    heads = SSM_HEADS
    hpg = heads // g
    c = pl.program_id(1)

    @pl.when(c == 0)
    def _():
        win_ref[0:8, :] = jnp.zeros((8, XBC_W), F32)
        state_ref[...] = jnp.zeros_like(state_ref)

    win_ref[8:8 + q, :] = xbc_ref[...].astype(F32)
    conv = convb_ref[...]
    for k in range(CONV_K):
        conv = conv + convw_ref[k:k + 1, :] * win_ref[8 - (CONV_K - 1) + k:8 - (CONV_K - 1) + k + q, :]
    win_ref[0:8, :] = win_ref[q:q + 8, :]
    xbc = conv * jax.nn.sigmoid(conv)
    xs = xbc[:, :D_INNER]
    bm = xbc[:, D_INNER:D_INNER + g * n]
    cm = xbc[:, D_INNER + g * n:]

    dt = jax.nn.softplus(dt_ref[...] + dtb_ref[...])
    a = dt * ahead_ref[...]
    row = lax.broadcasted_iota(jnp.int32, (q, q), 0)
    col = lax.broadcasted_iota(jnp.int32, (q, q), 1)
    causal = row >= col
    tril = jnp.where(causal, 1.0, 0.0).astype(BF16)
    a_cs = _split_dot_left(tril, a)
    a_cs_t = _transpose_heads(a_cs)
    dt_t = _transpose_heads(dt)
    a_last = a_cs[q - 1:q, :]
    decay_to_end = jnp.exp(a_last - a_cs)
    lane = lax.broadcasted_iota(jnp.int32, (heads, heads * hd), 1)
    first = lax.broadcasted_iota(jnp.int32, (heads, heads * hd), 0) * hd
    expand = jnp.where((lane >= first) & (lane < first + hd), 1.0, 0.0).astype(BF16)
    w_state = _split_dot(dt * decay_to_end, expand, 2)
    chunk_decay = _split_dot(jnp.broadcast_to(jnp.exp(a_last), (8, heads)), expand, 2)[0:1, :]

    prev = state_ref[...]
    prev_b = prev.astype(BF16)
    xs_b = xs.astype(BF16)
    ys = []
    for gi in range(g):
        bg = bm[:, gi * n:(gi + 1) * n].astype(BF16)
        cg = cm[:, gi * n:(gi + 1) * n]
        cb = lax.dot_general(cg.astype(BF16), bg, (((1,), (1,)), ((), ())), preferred_element_type=F32)
        for r in range(hpg):
            h = gi * hpg + r
            a_col = jnp.broadcast_to(a_cs[:, h:h + 1], (q, q))
            seg = a_col - a_cs_t[h:h + 1, :]
            decay = jnp.where(causal, jnp.exp(jnp.where(causal, seg, 0.0)), 0.0)
            m_diag = (cb * decay * dt_t[h:h + 1, :]).astype(BF16)
            m_off = (cg * jnp.exp(a_col)).astype(BF16)
            lhs = jnp.concatenate([m_diag, m_off], axis=1)
            rhs = jnp.concatenate([xs_b[:, h * hd:(h + 1) * hd], prev_b[:, h * hd:(h + 1) * hd]], axis=0)
            ys.append(jnp.dot(lhs, rhs, preferred_element_type=F32))
        xw = (xs[:, gi * hpg * hd:(gi + 1) * hpg * hd] * w_state[:, gi * hpg * hd:(gi + 1) * hpg * hd]).astype(BF16)
        s_new = lax.dot_general(bg, xw, (((0,), (0,)), ((), ())), preferred_element_type=F32)
        sl = slice(gi * hpg * hd, (gi + 1) * hpg * hd)
        state_ref[:, sl] = prev[:, sl] * chunk_decay[:, sl] + s_new
    y = jnp.concatenate(ys, axis=1) + dexp_ref[...] * xs

    z = z_ref[...].astype(F32)
    yg = y * (z * jax.nn.sigmoid(z))
    gw = D_INNER // g
    outs = []
    for gi in range(g):
        blk = yg[:, gi * gw:(gi + 1) * gw]
        ms = jnp.mean(blk * blk, axis=-1, keepdims=True)
        outs.append(blk * lax.rsqrt(ms + EPS))
    o_ref[...] = (jnp.concatenate(outs, axis=1) * normw_ref[...]).astype(o_ref.dtype)


def _transpose_heads(x):
    q, h = x.shape
    return jnp.concatenate([x, jnp.zeros((q, q - h), x.dtype)], axis=1).T[:h, :]


def _split_dot_left(b01, a):
    out = None
    rem = a
    for _ in range(3):
        hi = rem.astype(BF16)
        term = jnp.dot(b01, hi, preferred_element_type=F32)
        out = term if out is None else out + term
        rem = rem - hi.astype(F32)
    return out


def _ssd(xbc, z, dt_raw, conv_w, conv_b, dt_bias, a_log, d_skip, norm_w, bsz, seq):
    q = SSD_CHUNK
    nc = seq // q
    a_head = -jnp.exp(a_log.astype(F32)).reshape(1, SSM_HEADS)
    d_exp = jnp.repeat(d_skip.astype(F32), SSM_HEAD_DIM).reshape(1, D_INNER)
    row_spec = lambda n: pl.BlockSpec((q, n), lambda b, c: (b * nc + c, 0))
    return pl.pallas_call(
        _ssd_kernel,
        out_shape=jax.ShapeDtypeStruct((bsz * seq, D_INNER), BF16),
        grid=(bsz, nc),
        in_specs=[row_spec(XBC_W), row_spec(D_INNER), row_spec(SSM_HEADS),
                  _const_spec((CONV_K, XBC_W)), _const_spec((1, XBC_W)), _const_spec((1, SSM_HEADS)),
                  _const_spec((1, SSM_HEADS)), _const_spec((1, D_INNER)), _const_spec((1, D_INNER))],
        out_specs=row_spec(D_INNER),
        scratch_shapes=[pltpu.VMEM((q + 8, XBC_W), F32), pltpu.VMEM((SSM_STATE, D_INNER), F32)],
        compiler_params=_params("parallel", "arbitrary"),
        name="ssd",
    )(xbc, z, dt_raw, conv_w.astype(F32), conv_b.reshape(1, XBC_W).astype(F32),
      dt_bias.reshape(1, SSM_HEADS).astype(F32), a_head, d_exp, norm_w.reshape(1, D_INNER).astype(F32))


def _sortable_to_f32(s):
    bits = jnp.where(s < 0, s ^ jnp.int32(-2 ** 31), ~s)
    return lax.bitcast_convert_type(bits, F32)


def _tile_count(mask):
    ones = jnp.where(mask, 1.0, 0.0)
    return jnp.sum(ones.reshape(ones.shape[0] // 8, 8, ones.shape[1]), axis=0)


def _dsa_kernel(qa_ref, qb_ref, qia_ref, qib_ref, wa_ref, wb_ref, k_ref, ki_ref, vt_ref, oa_ref, ob_ref,
                sc_ref, nd_ref, m_ref, acc_ref, *, top_k, nq):
    tq = DSA_QBLOCK
    p = pl.program_id(1)
    n_a = p + 1
    nslot = nq + 1
    t_a = p * tq + lax.broadcasted_iota(jnp.int32, (1, tq), 1)
    t_b = (nq - 1 - p) * tq + lax.broadcasted_iota(jnp.int32, (1, tq), 1)
    key_in_tile = lax.broadcasted_iota(jnp.int32, (tq, tq), 0)
    kf = float(top_k)

    def score_block(qi_ref, w_ref, t_row, base, n_chunks):
        qi = qi_ref[...]
        qi_t = jnp.concatenate([qi[j * IDX_DIM:(j + 1) * IDX_DIM, :] for j in range(IDX_HEADS)], axis=1)
        w = w_ref[...] * (IDX_HEADS ** -0.5) * (IDX_DIM ** -0.5)

        def body(pair, carry):
            for u in range(2):
                c = jnp.minimum(2 * pair + u, n_chunks - 1)
                kc = ki_ref[pl.ds(pl.multiple_of(c * tq, tq), tq), :]
                rel = jnp.dot(kc, qi_t, preferred_element_type=F32)
                s = w[0:1, :] * jnp.maximum(rel[:, 0:tq], 0.0)
                for j in range(1, IDX_HEADS):
                    s = s + w[j:j + 1, :] * jnp.maximum(rel[:, j * tq:(j + 1) * tq], 0.0)
                sc_ref[base + c] = jnp.where(c * tq + key_in_tile <= t_row, s, -jnp.inf)
            return carry

        lax.fori_loop(0, (n_chunks + 1) // 2, body, 0)

    score_block(qia_ref, wa_ref, t_a, 0, n_a)
    score_block(qib_ref, wb_ref, t_b, n_a, nq - p)

    def both_counts(pred):
        acc_a = jnp.zeros((8, tq), F32)
        acc_b = jnp.zeros((8, tq), F32)
        for s in range(nslot):
            is_a = s <= p
            part = _tile_count(pred(s, sc_ref[s], is_a))
            acc_a = acc_a + jnp.where(is_a, part, 0.0)
            acc_b = acc_b + jnp.where(is_a, 0.0, part)
        return jnp.sum(acc_a, axis=0, keepdims=True), jnp.sum(acc_b, axis=0, keepdims=True)

    def value_step(b, carry):
        pre_a, pre_b = carry
        bit = lax.shift_left(jnp.int32(1), 31 - b)
        cand_a, cand_b = pre_a | bit, pre_b | bit
        cf_a, cf_b = _sortable_to_f32(cand_a), _sortable_to_f32(cand_b)
        cnt_a, cnt_b = both_counts(lambda s, tile, is_a: tile >= jnp.where(is_a, cf_a, cf_b))
        return jnp.where(cnt_a >= kf, cand_a, pre_a), jnp.where(cnt_b >= kf, cand_b, pre_b)

    zero_row = jnp.zeros((1, tq), jnp.int32)
    pre_a, pre_b = lax.fori_loop(0, 32, value_step, (zero_row, zero_row))
    thr_a, thr_b = _sortable_to_f32(pre_a), _sortable_to_f32(pre_b)
    above_a, above_b = both_counts(lambda s, tile, is_a: tile > jnp.where(is_a, thr_a, thr_b))
    need_a, need_b = kf - above_a, kf - above_b

    def slot_key(s, is_a):
        return jnp.where(is_a, s, s - n_a) * tq + key_in_tile

    def index_step(b, carry):
        lo_a, lo_b = carry
        bit = lax.shift_left(jnp.int32(1), b)
        cand_a, cand_b = lo_a + bit, lo_b + bit
        cnt_a, cnt_b = both_counts(
            lambda s, tile, is_a: (tile == jnp.where(is_a, thr_a, thr_b))
            & (slot_key(s, is_a) < jnp.where(is_a, cand_a, cand_b)))
        return jnp.where(cnt_a < need_a, cand_a, lo_a), jnp.where(cnt_b < need_b, cand_b, lo_b)

    tie_a, tie_b = both_counts(lambda s, tile, is_a: tile == jnp.where(is_a, thr_a, thr_b))
    excess = jnp.maximum(jnp.max(tie_a - need_a), jnp.max(tie_b - need_b))
    nbits = (nq * tq - 1).bit_length()
    all_keys = jnp.full((1, tq), nq * tq, jnp.int32)
    last_a, last_b = lax.cond(
        excess > 0.0,
        lambda: lax.fori_loop(0, nbits, lambda b, c: index_step(nbits - 1 - b, c), (zero_row, zero_row)),
        lambda: (all_keys, all_keys))

    def finish(t_row, thr, last):
        early = t_row < top_k
        return jnp.where(early, -jnp.finfo(F32).max, thr), jnp.where(early, nq * tq, last)

    thr_a, last_a = finish(t_a, thr_a, last_a)
    thr_b, last_b = finish(t_b, thr_b, last_b)

    for s in range(nslot):
        is_a = s <= p
        thr = jnp.where(is_a, thr_a, thr_b)
        key = slot_key(s, is_a)
        tile = sc_ref[s]
        sel = (tile > thr) | ((tile == thr) & (key <= jnp.where(is_a, last_a, last_b)))
        nd_ref[s] = jnp.where(sel, (key - jnp.where(is_a, t_a, t_b)).astype(F32), MASK_NEG)

    rep = ATT_HEADS // ATT_KV_HEADS
    hd = ATT_HEAD_DIM
    ones_rows = jnp.where(lax.broadcasted_iota(jnp.int32, (16, tq), 0) == 0, 1.0, 0.0).astype(BF16)

    def attend_block(q_ref, o_ref, base, n_chunks):
        q_ts = []
        for g in range(ATT_KV_HEADS):
            q_t = jnp.concatenate([q_ref[(g * rep + r) * hd:(g * rep + r + 1) * hd, :] for r in range(rep)], axis=1)
            q_ts.append(q_t * (hd ** -0.5))
        m_ref[...] = jnp.full(m_ref.shape, -jnp.inf, F32)
        acc_ref[...] = jnp.zeros(acc_ref.shape, F32)

        def body(c, carry):
            kc_all = k_ref[pl.ds(pl.multiple_of(c * tq, tq), tq), :]
            nd = nd_ref[base + c]
            raw = [jnp.dot(kc_all[:, g * hd:(g + 1) * hd], q_ts[g], preferred_element_type=F32)
                   for g in range(ATT_KV_HEADS)]
            for g in range(ATT_KV_HEADS):
                logits = raw[g]
                logits = jnp.concatenate(
                    [logits[:, r * tq:(r + 1) * tq] + (2.0 ** (-ALIBI_BASE * (g * rep + r + 1) / ATT_HEADS)) * nd
                     for r in range(rep)], axis=1)
                m_old = m_ref[g]
                m_new = jnp.maximum(m_old, jnp.max(logits, axis=0, keepdims=True))
                prob = jnp.exp(logits - m_new).astype(BF16)
                v_aug = jnp.concatenate([vt_ref[c, g * hd:(g + 1) * hd, :], ones_rows], axis=0)
                pv = jnp.dot(v_aug, prob, preferred_element_type=F32)
                acc_ref[g] = acc_ref[g] * jnp.exp(m_old - m_new) + pv
                m_ref[g] = m_new
            return carry

        lax.fori_loop(0, n_chunks, body, 0)
        outs = []
        for g in range(ATT_KV_HEADS):
            acc = acc_ref[g]
            o_t = acc[:hd, :] / acc[hd:hd + 1, :]
            outs.extend(o_t[:, r * tq:(r + 1) * tq] for r in range(rep))
        o_ref[...] = jnp.concatenate(outs, axis=0).T.astype(o_ref.dtype)

    attend_block(qa_ref, oa_ref, 0, n_a)
    attend_block(qb_ref, ob_ref, n_a, nq - p)


def _dsa(q_t, k, v_t, qi_t, k_idx, wi_t, bsz, seq):
    tq = DSA_QBLOCK
    nq = seq // tq
    half = nq // 2
    top_k = min(TOPK_MAX, seq // 4)
    width = ATT_HEADS * ATT_HEAD_DIM
    col_a = lambda n: pl.BlockSpec((n, tq), lambda b, p: (0, b * nq + p))
    col_b = lambda n: pl.BlockSpec((n, tq), lambda b, p: (0, b * nq + nq - 1 - p))
    full = lambda n: pl.BlockSpec((seq, n), lambda b, p: (b, 0))
    out_a, out_b = pl.pallas_call(
        functools.partial(_dsa_kernel, top_k=top_k, nq=nq),
        out_shape=[jax.ShapeDtypeStruct((bsz, half * tq, width), BF16)] * 2,
        grid=(bsz, half),
        in_specs=[col_a(q_t.shape[0]), col_b(q_t.shape[0]), col_a(qi_t.shape[0]), col_b(qi_t.shape[0]),
                  col_a(wi_t.shape[0]), col_b(wi_t.shape[0]), full(k.shape[1]), full(k_idx.shape[1]),
                  pl.BlockSpec((nq, v_t.shape[1], tq), lambda b, p: (b, 0, 0))],
        out_specs=[pl.BlockSpec((None, tq, width), lambda b, p: (b, p, 0)),
                   pl.BlockSpec((None, tq, width), lambda b, p: (b, half - 1 - p, 0))],
        scratch_shapes=[pltpu.VMEM((nq + 1, tq, tq), F32), pltpu.VMEM((nq + 1, tq, tq), F32),
                        pltpu.VMEM((ATT_KV_HEADS, 1, (ATT_HEADS // ATT_KV_HEADS) * tq), F32),
                        pltpu.VMEM((ATT_KV_HEADS, ATT_HEAD_DIM + 16, (ATT_HEADS // ATT_KV_HEADS) * tq), F32)],
        compiler_params=_params("parallel", "arbitrary"),
        name="dsa",
    )(q_t, q_t, qi_t, qi_t, wi_t, wi_t, k, k_idx, v_t)
    return jnp.concatenate([out_a, out_b], axis=1).reshape(bsz * seq, width)


def _merge_kernel(x_ref, ya_ref, yb_ref, ga_ref, gb_ref, wa_ref, wb_ref, wo_ref, o_ref):
    pa = jnp.dot(ya_ref[...], wa_ref[...], preferred_element_type=F32)
    pb = jnp.dot(yb_ref[...], wb_ref[...], preferred_element_type=F32)
    merged = (jax.nn.sigmoid(ga_ref[...].astype(F32)) * pa + jax.nn.sigmoid(gb_ref[...].astype(F32)) * pb)
    o_ref[...] = x_ref[...] + jnp.dot(merged.astype(BF16), wo_ref[...], preferred_element_type=F32)


def _merge(x1, y_ssm, y_att, g_ssm, g_att, w_a, w_b, w_o):
    t, d = x1.shape
    rows = min(MERGE_ROWS, t)
    row_spec = lambda n: pl.BlockSpec((rows, n), lambda i: (i, 0))
    return pl.pallas_call(
        _merge_kernel,
        out_shape=jax.ShapeDtypeStruct((t, d), F32),
        grid=(t // rows,),
        in_specs=[row_spec(d), row_spec(y_ssm.shape[1]), row_spec(y_att.shape[1]), row_spec(d), row_spec(d),
                  _const_spec(w_a.shape), _const_spec(w_b.shape), _const_spec(w_o.shape)],
        out_specs=row_spec(d),
        compiler_params=_params("parallel"),
        name="merge",
    )(x1, y_ssm, y_att, g_ssm, g_att, w_a, w_b, w_o)


def kernel(x, ffn1_norm, ffn1_w_gate, ffn1_w_up, ffn1_w_down, mix_norm, w_in, conv_w, conv_b, dt_bias, a_log,
           d_skip, ssm_norm, w_branch_ssm, w_branch_attn, w_out, ffn2_norm, ffn2_w_gate, ffn2_w_up, ffn2_w_down,
           final_norm):
    bsz, seq, d = x.shape
    depth = ffn1_norm.shape[0]
    h = x.reshape(bsz * seq, d)
    for l in range(depth):
        last = l == depth - 1
        h = _ffn(h, ffn1_norm[l], ffn1_w_gate[l].astype(BF16), ffn1_w_up[l].astype(BF16),
                 ffn1_w_down[l].astype(BF16))
        z, xbc, k, g_ssm, g_att, dt_raw, k_idx, q_t, qi_t, v_t, wi_t = _in_proj(h, mix_norm[l], w_in[l])
        y_ssm = _ssd(xbc, z, dt_raw, conv_w[l], conv_b[l], dt_bias[l], a_log[l], d_skip[l], ssm_norm[l], bsz, seq)
        y_att = _dsa(q_t, k, v_t, qi_t, k_idx, wi_t, bsz, seq)
        h = _merge(h, y_ssm, y_att, g_ssm, g_att, w_branch_ssm[l].astype(BF16), w_branch_attn[l].astype(BF16),
                   w_out[l].astype(BF16))
        h = _ffn(h, ffn2_norm[l], ffn2_w_gate[l].astype(BF16), ffn2_w_up[l].astype(BF16),
                 ffn2_w_down[l].astype(BF16), final_g=final_norm if last else None)
    return h.reshape(bsz, seq, d)
```

```python
import functools

import jax
import jax.numpy as jnp
import numpy as np
from jax import lax
from jax.experimental import pallas as pl
from jax.experimental.pallas import tpu as pltpu

F32 = jnp.float32
BF16 = jnp.bfloat16

D_MODEL = 1024
D_INNER = 2048
SSM_HEAD_DIM = 64
SSM_HEADS = 32
SSM_GROUPS = 4
SSM_STATE = 128
CONV_K = 4
XBC_W = D_INNER + 2 * SSM_GROUPS * SSM_STATE
ATT_HEADS = 16
ATT_HEAD_DIM = 64
ATT_KV_HEADS = 4
IDX_HEADS = 8
IDX_DIM = 64
TOPK_MAX = 256
ALIBI_BASE = 8.0
D_FF = 2816
FFN_RESIDUAL_SCALE = 0.5
EPS = 1e-6

IN_SIZES = (D_INNER, XBC_W, SSM_HEADS, ATT_HEADS * ATT_HEAD_DIM, ATT_KV_HEADS * ATT_HEAD_DIM,
            ATT_KV_HEADS * ATT_HEAD_DIM, IDX_HEADS * IDX_DIM, IDX_DIM, IDX_HEADS, D_MODEL, D_MODEL)

VMEM_LIMIT_BYTES = 56 * 1024 * 1024
LOG2E = 1.4426950408889634
SSD_CHUNK = 128
CONV_HALO = 16
SSD_CONV_COLS = 512
DSA_QBLOCK = 128
KAUG_W = 128
FFN_ROWS = 512
FFN_FCHUNK = 256
PROJ_ROWS = 256
MERGE_ROWS = 512
MASK_NEG = -1e30


def _rms(x, g):
    ms = jnp.mean(x * x, axis=-1, keepdims=True)
    return x * lax.rsqrt(ms + EPS) * g


def _silu(x):
    return x / (1.0 + jnp.exp2(x * -LOG2E))


def _const_spec(shape):
    return pl.BlockSpec(shape, lambda *_: (0,) * len(shape), pipeline_mode=pl.Buffered(1))


def _params(*sem):
    return pltpu.CompilerParams(dimension_semantics=sem, vmem_limit_bytes=VMEM_LIMIT_BYTES)


def _ffn_kernel(x_ref, g_ref, wg_ref, wu_ref, wd_ref, *rest, final_norm):
    if final_norm:
        fg_ref, o_ref, acc_ref = rest
    else:
        o_ref, acc_ref = rest
    x = x_ref[...]
    xn = _rms(x, g_ref[...]).astype(BF16)
    d_ff = wg_ref.shape[1]
    for c in range(0, d_ff, FFN_FCHUNK):
        gate = jnp.dot(xn, wg_ref[:, c:c + FFN_FCHUNK], preferred_element_type=F32)
        up = jnp.dot(xn, wu_ref[:, c:c + FFN_FCHUNK], preferred_element_type=F32)
        act = (gate * jax.nn.sigmoid(gate) * up).astype(BF16)
        part = jnp.dot(act, wd_ref[c:c + FFN_FCHUNK, :], preferred_element_type=F32)
        if c == 0:
            acc_ref[...] = part
        else:
            acc_ref[...] += part
    y = x + FFN_RESIDUAL_SCALE * acc_ref[...]
    if final_norm:
        y = _rms(y, fg_ref[...])
    o_ref[...] = y


def _ffn(x, norm_g, wg, wu, wd, final_g=None):
    t, d = x.shape
    d_ff = wg.shape[1]
    rows = min(FFN_ROWS, t)
    in_specs = [pl.BlockSpec((rows, d), lambda i: (i, 0)), _const_spec((1, d)),
                _const_spec((d, d_ff)), _const_spec((d, d_ff)), _const_spec((d_ff, d))]
    args = [x, norm_g.reshape(1, d), wg, wu, wd]
    if final_g is not None:
        in_specs.append(_const_spec((1, d)))
        args.append(final_g.reshape(1, d))
    return pl.pallas_call(
        functools.partial(_ffn_kernel, final_norm=final_g is not None),
        out_shape=jax.ShapeDtypeStruct((t, d), F32),
        grid=(t // rows,),
        in_specs=in_specs,
        out_specs=pl.BlockSpec((rows, d), lambda i: (i, 0)),
        scratch_shapes=[pltpu.VMEM((rows, d), F32)],
        compiler_params=_params("parallel"),
        name="ffn_final" if final_g is not None else "ffn",
    )(*args)


def _dot_nt(a, b):
    return lax.dot_general(a, b, (((1,), (1,)), ((), ())), preferred_element_type=F32)


def _in_proj_kernel(x_ref, g_ref, wz, wxbc, wk, wgs, wga, wmisc, wqt, wqit, wvt, wwit,
                    oz, oxbc, ok, ogs, oga, odt, oki, oqt, oqit, ovt, owit, *, seq):
    xn = _rms(x_ref[...], g_ref[...]).astype(BF16)
    for w_ref, o_ref in ((wz, oz), (wxbc, oxbc), (wgs, ogs), (wga, oga)):
        n = w_ref.shape[1]
        step = min(n, 512)
        for c in range(0, n, step):
            o_ref[:, c:c + step] = jnp.dot(xn, w_ref[:, c:c + step], preferred_element_type=F32).astype(o_ref.dtype)
    rows = x_ref.shape[0]
    pos = (pl.program_id(0) * rows) % seq + lax.broadcasted_iota(jnp.int32, (rows, KAUG_W), 0)
    lane = lax.broadcasted_iota(jnp.int32, (rows, KAUG_W), 1) - ATT_HEAD_DIM
    chunk_f = lax.shift_right_logical(pos, DSA_QBLOCK.bit_length() - 1).astype(F32)
    row_f = (pos & (DSA_QBLOCK - 1)).astype(F32)
    pos_cols = jnp.where((lane >= 0) & (lane < 3), chunk_f,
                         jnp.where((lane >= 3) & (lane < 6), row_f,
                                   jnp.where((lane >= 6) & (lane < 9), 1.0, 0.0)))
    for g in range(ATT_KV_HEADS):
        cols = slice(g * KAUG_W, (g + 1) * KAUG_W)
        ok[:, cols] = (jnp.dot(xn, wk[:, cols], preferred_element_type=F32) + pos_cols).astype(BF16)
    misc = jnp.dot(xn, wmisc[...], preferred_element_type=F32)
    odt[...] = misc[:, :SSM_HEADS]
    oki[...] = misc[:, SSM_HEADS:SSM_HEADS + IDX_DIM].astype(BF16)
    for w_ref, o_ref in ((wqt, oqt), (wqit, oqit)):
        n = w_ref.shape[0]
        for c in range(0, n, 256):
            o_ref[c:c + 256, :] = _dot_nt(w_ref[c:c + 256, :], xn).astype(o_ref.dtype)
    owit[...] = _dot_nt(wwit[...], xn)[:IDX_HEADS, :]
    vt = _dot_nt(wvt[...], xn).astype(BF16)
    for j in range(ovt.shape[0]):
        ovt[j] = vt[:, j * DSA_QBLOCK:(j + 1) * DSA_QBLOCK]


def _in_proj(x1, norm_g, w_in, seq):
    t, d = x1.shape
    offs = [0]
    for s in IN_SIZES:
        offs.append(offs[-1] + s)
    seg = lambda i: w_in[:, offs[i]:offs[i + 1]]
    w_misc = jnp.concatenate([seg(2), seg(7), jnp.zeros((d, 128 - SSM_HEADS - IDX_DIM), w_in.dtype)], axis=1)
    w_k = seg(4).reshape(d, ATT_KV_HEADS, ATT_HEAD_DIM)
    w_k = jnp.pad(w_k, ((0, 0), (0, 0), (0, KAUG_W - ATT_HEAD_DIM))).reshape(d, ATT_KV_HEADS * KAUG_W)
    natural = [seg(0), seg(1), w_k, seg(9), seg(10), w_misc]
    w_idx_t = jnp.concatenate([seg(8).T, jnp.zeros((16 - IDX_HEADS, d), w_in.dtype)], axis=0)
    transposed = [seg(3).T, seg(6).T, seg(5).T, w_idx_t]
    weights = [w.astype(BF16) for w in natural + transposed]
    rows = min(PROJ_ROWS, t)
    nchunk = rows // DSA_QBLOCK
    row_spec = lambda n: pl.BlockSpec((rows, n), lambda i: (i, 0))
    col_spec = lambda n: pl.BlockSpec((n, rows), lambda i: (0, i))
    kv_w = ATT_KV_HEADS * ATT_HEAD_DIM
    out_shapes = [jax.ShapeDtypeStruct((t, w.shape[1]), BF16) for w in natural[:-1]]
    out_shapes += [jax.ShapeDtypeStruct((t, SSM_HEADS), F32), jax.ShapeDtypeStruct((t, IDX_DIM), BF16),
                   jax.ShapeDtypeStruct((ATT_HEADS * ATT_HEAD_DIM, t), BF16),
                   jax.ShapeDtypeStruct((IDX_HEADS * IDX_DIM, t), BF16),
                   jax.ShapeDtypeStruct((t // DSA_QBLOCK, kv_w, DSA_QBLOCK), BF16),
                   jax.ShapeDtypeStruct((IDX_HEADS, t), F32)]
    out_specs = [row_spec(s.shape[1]) for s in out_shapes[:7]]
    out_specs += [col_spec(ATT_HEADS * ATT_HEAD_DIM), col_spec(IDX_HEADS * IDX_DIM),
                  pl.BlockSpec((nchunk, kv_w, DSA_QBLOCK), lambda i: (i, 0, 0)), col_spec(IDX_HEADS)]
    return pl.pallas_call(
        functools.partial(_in_proj_kernel, seq=seq),
        out_shape=out_shapes,
        grid=(t // rows,),
        in_specs=[row_spec(d), _const_spec((1, d))] + [_const_spec(w.shape) for w in weights],
        out_specs=out_specs,
        compiler_params=_params("parallel"),
        name="in_proj",
    )(x1, norm_g.reshape(1, d), *weights)


def _ssd_kernel(xbc_ref, z_ref, dt_ref, shift_ref, convw_ref, convb_ref, dtb_ref, ahead2_ref, dexp_ref, normw_ref,
                o_ref, win_ref, state_ref):
    q = SSD_CHUNK
    hd, n, g = SSM_HEAD_DIM, SSM_STATE, SSM_GROUPS
    heads = SSM_HEADS
    hpg = heads // g
    gw = hpg * hd
    c = pl.program_id(1)

    @pl.when(c == 0)
    def _():
        win_ref[0:CONV_HALO, :] = jnp.zeros((CONV_HALO, XBC_W), BF16)
        state_ref[...] = jnp.zeros_like(state_ref)

    x_cur = xbc_ref[...]
    win_ref[CONV_HALO:CONV_HALO + q, :] = x_cur
    pieces = []
    for c0 in range(0, XBC_W, SSD_CONV_COLS):
        cols = slice(c0, c0 + SSD_CONV_COLS)
        shifted = jnp.dot(shift_ref[...], win_ref[:, cols], preferred_element_type=F32)
        conv = convb_ref[:, cols] + convw_ref[CONV_K - 1:CONV_K, cols] * x_cur[:, cols].astype(F32)
        for k in range(CONV_K - 1):
            conv = conv + convw_ref[k:k + 1, cols] * shifted[k * q:(k + 1) * q, :]
        pieces.append(_silu(conv))
    win_ref[0:CONV_HALO, :] = x_cur[q - CONV_HALO:, :]
    xbc = jnp.concatenate(pieces, axis=1)
    xs = xbc[:, :D_INNER]
    bm = xbc[:, D_INNER:D_INNER + g * n]
    cm = xbc[:, D_INNER + g * n:]

    dt = jax.nn.softplus(dt_ref[...] + dtb_ref[...])
    a2 = dt * ahead2_ref[...]
    row = lax.broadcasted_iota(jnp.int32, (q, q), 0)
    col = lax.broadcasted_iota(jnp.int32, (q, q), 1)
    causal = row >= col
    tril = jnp.where(causal, 1.0, 0.0).astype(BF16)
    a2_cs = _split_dot_left(tril, a2)
    a2_cs_t = _transpose_heads(a2_cs)
    a2_last = a2_cs[q - 1:q, :]
    stack = jnp.concatenate([dt, jnp.exp2(a2_cs), dt * jnp.exp2(a2_last - a2_cs),
                             jnp.broadcast_to(jnp.exp2(a2_last), (16, heads))], axis=0)
    hi = stack.astype(BF16)
    lo = (stack - hi.astype(F32)).astype(BF16)
    lane = lax.broadcasted_iota(jnp.int32, (2 * heads, heads * hd), 1)
    src = lax.broadcasted_iota(jnp.int32, (2 * heads, heads * hd), 0)
    first = jnp.where(src >= heads, src - heads, src) * hd
    expand = jnp.where((lane >= first) & (lane < first + hd), 1.0, 0.0).astype(BF16)
    ex = jnp.dot(jnp.concatenate([hi, lo], axis=1), expand, preferred_element_type=F32)
    dt_x, in_decay_x, w_state_x, chunk_decay_x = ex[0:q], ex[q:2 * q], ex[2 * q:3 * q], ex[3 * q:3 * q + 1]

    prev = state_ref[...]
    prev_b = prev.astype(BF16)
    lane_pair = lax.broadcasted_iota(jnp.int32, (q, 2 * hd), 1)
    ys = []
    for gi in range(g):
        sl = slice(gi * gw, (gi + 1) * gw)
        bg = bm[:, gi * n:(gi + 1) * n]
        bg_b = bg.astype(BF16)
        cg_b = cm[:, gi * n:(gi + 1) * n].astype(BF16)
        cb = jnp.where(causal, _dot_nt(cg_b, bg_b), 0.0)
        xs_g = xs[:, sl]
        xdt_b = (xs_g * dt_x[:, sl]).astype(BF16)
        y_off = jnp.dot(cg_b, prev_b[:, sl], preferred_element_type=F32)
        pairs = []
        for pr in range(hpg // 2):
            ms = []
            for h in (gi * hpg + 2 * pr, gi * hpg + 2 * pr + 1):
                seg = jnp.broadcast_to(a2_cs[:, h:h + 1], (q, q)) - a2_cs_t[h:h + 1, :]
                ms.append((cb * jnp.exp2(jnp.minimum(seg, 0.0))).astype(BF16))
            xp = xdt_b[:, 2 * pr * hd:(2 * pr + 2) * hd]
            zero = jnp.zeros_like(xp)
            rhs = jnp.concatenate([jnp.where(lane_pair < hd, xp, zero), jnp.where(lane_pair >= hd, xp, zero)], axis=0)
            pairs.append(jnp.dot(jnp.concatenate(ms, axis=1), rhs, preferred_element_type=F32))
        ys.append(jnp.concatenate(pairs, axis=1) + y_off * in_decay_x[:, sl] + dexp_ref[:, sl] * xs_g)
        xw = (xs_g * w_state_x[:, sl]).astype(BF16)
        s_new = jnp.dot(bg.T.astype(BF16), xw, preferred_element_type=F32)
        state_ref[:, sl] = prev[:, sl] * chunk_decay_x[:, sl] + s_new
    y = jnp.concatenate(ys, axis=1)

    z = z_ref[...].astype(F32)
    yg = y * _silu(z)
    outs = []
    for gi in range(g):
        blk = yg[:, gi * gw:(gi + 1) * gw]
        ms = jnp.mean(blk * blk, axis=-1, keepdims=True)
        outs.append(blk * lax.rsqrt(ms + EPS))
    o_ref[...] = (jnp.concatenate(outs, axis=1) * normw_ref[...]).astype(o_ref.dtype)


def _transpose_heads(x):
    q, h = x.shape
    return jnp.concatenate([x, jnp.zeros((q, q - h), x.dtype)], axis=1).T[:h, :]


def _split_dot_left(b01, a):
    out = None
    rem = a
    for _ in range(3):
        hi = rem.astype(BF16)
        term = jnp.dot(b01, hi, preferred_element_type=F32)
        out = term if out is None else out + term
        rem = rem - hi.astype(F32)
    return out


def _conv_shift_matrix(q):
    s = np.zeros(((CONV_K - 1) * q, CONV_HALO + q), np.float32)
    for k in range(CONV_K - 1):
        s[k * q + np.arange(q), CONV_HALO + np.arange(q) - (CONV_K - 1) + k] = 1.0
    return jnp.asarray(s, BF16)


def _ssd(xbc, z, dt_raw, conv_w, conv_b, dt_bias, a_log, d_skip, norm_w, bsz, seq):
    q = SSD_CHUNK
    nc = seq // q
    a_head2 = (-jnp.exp(a_log.astype(F32)) * LOG2E).reshape(1, SSM_HEADS)
    d_exp = jnp.repeat(d_skip.astype(F32), SSM_HEAD_DIM).reshape(1, D_INNER)
    shift = _conv_shift_matrix(q)
    row_spec = lambda n: pl.BlockSpec((q, n), lambda b, c: (b * nc + c, 0))
    return pl.pallas_call(
        _ssd_kernel,
        out_shape=jax.ShapeDtypeStruct((bsz * seq, D_INNER), BF16),
        grid=(bsz, nc),
        in_specs=[row_spec(XBC_W), row_spec(D_INNER), row_spec(SSM_HEADS), _const_spec(shift.shape),
                  _const_spec((CONV_K, XBC_W)), _const_spec((1, XBC_W)), _const_spec((1, SSM_HEADS)),
                  _const_spec((1, SSM_HEADS)), _const_spec((1, D_INNER)), _const_spec((1, D_INNER))],
        out_specs=row_spec(D_INNER),
        scratch_shapes=[pltpu.VMEM((CONV_HALO + q, XBC_W), BF16), pltpu.VMEM((SSM_STATE, D_INNER), F32)],
        compiler_params=_params("parallel", "arbitrary"),
        name="ssd",
    )(xbc, z, dt_raw, shift, conv_w.astype(F32), conv_b.reshape(1, XBC_W).astype(F32),
      dt_bias.reshape(1, SSM_HEADS).astype(F32), a_head2, d_exp, norm_w.reshape(1, D_INNER).astype(F32))


def _sortable_to_f32(s):
    bits = jnp.where(s < 0, s ^ jnp.int32(-2 ** 31), ~s)
    return lax.bitcast_convert_type(bits, F32)


def _tile_count(mask):
    ones = jnp.where(mask, 1.0, 0.0)
    return jnp.sum(ones.reshape(ones.shape[0] // 8, 8, ones.shape[1]), axis=0)


def _np_bf16_pieces(x, n=3):
    out = []
    rem = np.float32(x)
    for _ in range(n):
        piece = np.float32(np.asarray(rem, dtype=jnp.bfloat16))
        out.append(piece)
        rem = np.float32(rem - piece)
    return out


def _bf16_pieces(x, n=3):
    out = []
    rem = x
    for _ in range(n):
        piece = rem.astype(BF16).astype(F32)
        out.append(piece)
        rem = rem - piece
    return out


def _dsa_kernel(qa_ref, qb_ref, qia_ref, qib_ref, wa_ref, wb_ref, k_ref, ki_ref, vt_ref, oa_ref, ob_ref,
                sc_ref, mask_ref, m_ref, acc_ref, qaug_ref, qis_ref, ws_ref, rel0_ref, rel1_ref, raw0_ref, raw1_ref,
                *, top_k, nq):
    tq = DSA_QBLOCK
    rep = ATT_HEADS // ATT_KV_HEADS
    hd = ATT_HEAD_DIM
    p = pl.program_id(1)
    n_a = p + 1
    nslot = nq + 1
    t_a = p * tq + lax.broadcasted_iota(jnp.int32, (1, tq), 1)
    t_b = (nq - 1 - p) * tq + lax.broadcasted_iota(jnp.int32, (1, tq), 1)
    key_in_tile = lax.broadcasted_iota(jnp.int32, (tq, tq), 0)
    kf = float(top_k)

    def slot_info(s):
        is_a = s <= p
        return jnp.where(is_a, 0, 1), jnp.minimum(jnp.where(is_a, s, s - n_a), nq - 1)

    for blk, (qi_ref, w_ref, q_ref, t_row) in enumerate(((qia_ref, wa_ref, qa_ref, t_a), (qib_ref, wb_ref, qb_ref, t_b))):
        qi = qi_ref[...]
        qis_ref[blk] = jnp.concatenate([qi[j * IDX_DIM:(j + 1) * IDX_DIM, :] for j in range(IDX_HEADS)], axis=1)
        ws_ref[blk] = w_ref[...] * (IDX_HEADS ** -0.5) * (IDX_DIM ** -0.5)
        row16 = lax.broadcasted_iota(jnp.int32, (16, tq), 0)
        for g in range(ATT_KV_HEADS):
            q_t = jnp.concatenate([q_ref[(g * rep + r) * hd:(g * rep + r + 1) * hd, :] for r in range(rep)], axis=1)
            bias_rows = []
            for r in range(rep):
                slope = np.float32(2.0 ** (-ALIBI_BASE * (g * rep + r + 1) / ATT_HEADS))
                pieces = [float(v) for v in _np_bf16_pieces(slope)]
                terms = [tq * v for v in pieces] + pieces + _bf16_pieces(-(slope * t_row.astype(F32)))
                rows = jnp.zeros((16, tq), F32)
                for i, term in enumerate(terms):
                    rows = jnp.where(row16 == i, term, rows)
                bias_rows.append(rows)
            qaug_ref[blk * ATT_KV_HEADS + g] = jnp.concatenate(
                [q_t * (hd ** -0.5), jnp.concatenate(bias_rows, axis=1).astype(BF16),
                 jnp.zeros((KAUG_W - hd - 16, rep * tq), BF16)], axis=0)

    def issue_rel(s, rel_ref):
        blk, c = slot_info(s)
        kc = ki_ref[pl.ds(pl.multiple_of(c * tq, tq), tq), :]
        rel_ref[...] = jnp.dot(kc, qis_ref[blk], preferred_element_type=F32)

    def score_slot(s, rel_ref):
        blk, c = slot_info(s)
        w = ws_ref[blk]
        acc = w[0:1, :] * jnp.maximum(rel_ref[:, 0:tq], 0.0)
        for j in range(1, IDX_HEADS):
            acc = acc + w[j:j + 1, :] * jnp.maximum(rel_ref[:, j * tq:(j + 1) * tq], 0.0)
        sc_ref[s] = jnp.where(c * tq + key_in_tile <= jnp.where(blk == 0, t_a, t_b), acc, -jnp.inf)

    issue_rel(0, rel0_ref)

    def score_trip(j, carry):
        s0 = 2 * j
        s1 = jnp.minimum(s0 + 1, nslot - 1)
        issue_rel(s1, rel1_ref)
        score_slot(s0, rel0_ref)
        issue_rel(jnp.minimum(s0 + 2, nslot - 1), rel0_ref)
        score_slot(s1, rel1_ref)
        return carry

    lax.fori_loop(0, (nslot + 1) // 2, score_trip, 0, unroll=True)

    def both_counts(pred):
        acc_a = jnp.zeros((8, tq), F32)
        acc_b = jnp.zeros((8, tq), F32)
        for s in range(nslot):
            is_a = s <= p
            part = _tile_count(pred(s, sc_ref[s], is_a))
            acc_a = acc_a + jnp.where(is_a, part, 0.0)
            acc_b = acc_b + jnp.where(is_a, 0.0, part)
        return jnp.sum(acc_a, axis=0, keepdims=True), jnp.sum(acc_b, axis=0, keepdims=True)

    def value_step(b, carry):
        pre_a, pre_b = carry
        bit = lax.shift_left(jnp.int32(1), 31 - b)
        cand_a, cand_b = pre_a | bit, pre_b | bit
        cf_a, cf_b = _sortable_to_f32(cand_a), _sortable_to_f32(cand_b)
        cnt_a, cnt_b = both_counts(lambda s, tile, is_a: tile >= jnp.where(is_a, cf_a, cf_b))
        return jnp.where(cnt_a >= kf, cand_a, pre_a), jnp.where(cnt_b >= kf, cand_b, pre_b)

    zero_row = jnp.zeros((1, tq), jnp.int32)
    pre_a, pre_b = lax.fori_loop(0, 32, value_step, (zero_row, zero_row))
    thr_a, thr_b = _sortable_to_f32(pre_a), _sortable_to_f32(pre_b)
    above_a, above_b = both_counts(lambda s, tile, is_a: tile > jnp.where(is_a, thr_a, thr_b))
    need_a, need_b = kf - above_a, kf - above_b

    def slot_key(s, is_a):
        return jnp.where(is_a, s, s - n_a) * tq + key_in_tile

    def index_step(b, carry):
        lo_a, lo_b = carry
        bit = lax.shift_left(jnp.int32(1), b)
        cand_a, cand_b = lo_a + bit, lo_b + bit
        cnt_a, cnt_b = both_counts(
            lambda s, tile, is_a: (tile == jnp.where(is_a, thr_a, thr_b))
            & (slot_key(s, is_a) < jnp.where(is_a, cand_a, cand_b)))
        return jnp.where(cnt_a < need_a, cand_a, lo_a), jnp.where(cnt_b < need_b, cand_b, lo_b)

    tie_a, tie_b = both_counts(lambda s, tile, is_a: tile == jnp.where(is_a, thr_a, thr_b))
    excess = jnp.maximum(jnp.max(tie_a - need_a), jnp.max(tie_b - need_b))
    nbits = (nq * tq - 1).bit_length()
    all_keys = jnp.full((1, tq), nq * tq, jnp.int32)
    last_a, last_b = lax.cond(
        excess > 0.0,
        lambda: lax.fori_loop(0, nbits, lambda b, c: index_step(nbits - 1 - b, c), (zero_row, zero_row)),
        lambda: (all_keys, all_keys))

    def finish(t_row, thr, last):
        early = t_row < top_k
        return jnp.where(early, -jnp.finfo(F32).max, thr), jnp.where(early, nq * tq, last)

    thr_a, last_a = finish(t_a, thr_a, last_a)
    thr_b, last_b = finish(t_b, thr_b, last_b)

    for s in range(nslot):
        is_a = s <= p
        thr = jnp.where(is_a, thr_a, thr_b)
        tile = sc_ref[s]
        sel = (tile > thr) | ((tile == thr) & (slot_key(s, is_a) <= jnp.where(is_a, last_a, last_b)))
        mask_ref[s] = jnp.where(sel, 0.0, MASK_NEG)
    mask_ref[nslot] = jnp.full((tq, tq), MASK_NEG, F32)

    ones_rows = jnp.where(lax.broadcasted_iota(jnp.int32, (16, tq), 0) == 0, 1.0, 0.0).astype(BF16)
    m_ref[...] = jnp.full(m_ref.shape, -jnp.inf, F32)
    acc_ref[...] = jnp.zeros(acc_ref.shape, F32)

    def issue_qk(s, raw_ref):
        blk, c = slot_info(s)
        kc = k_ref[pl.ds(pl.multiple_of(c * tq, tq), tq), :]
        for g in range(ATT_KV_HEADS):
            raw_ref[g] = jnp.dot(kc[:, g * KAUG_W:(g + 1) * KAUG_W], qaug_ref[blk * ATT_KV_HEADS + g],
                                 preferred_element_type=F32)

    def attend_slot(s, raw_ref):
        blk, c = slot_info(s)
        mask = mask_ref[s]
        for g in range(ATT_KV_HEADS):
            slot = blk * ATT_KV_HEADS + g
            logits = jnp.concatenate([raw_ref[g, :, r * tq:(r + 1) * tq] + mask for r in range(rep)], axis=1)
            m_old = m_ref[slot]
            m_new = jnp.maximum(m_old, jnp.max(logits, axis=0, keepdims=True))
            prob = jnp.exp(logits - m_new).astype(BF16)
            v_aug = jnp.concatenate([vt_ref[c, g * hd:(g + 1) * hd, :], ones_rows], axis=0)
            pv = jnp.dot(v_aug, prob, preferred_element_type=F32)
            acc_ref[slot] = acc_ref[slot] * jnp.exp(m_old - m_new) + pv
            m_ref[slot] = m_new

    issue_qk(0, raw0_ref)

    def attend_trip(j, carry):
        s0 = 2 * j
        issue_qk(s0 + 1, raw1_ref)
        attend_slot(s0, raw0_ref)
        issue_qk(jnp.minimum(s0 + 2, nslot), raw0_ref)
        attend_slot(s0 + 1, raw1_ref)
        return carry

    lax.fori_loop(0, (nslot + 1) // 2, attend_trip, 0, unroll=True)

    for blk, o_ref in enumerate((oa_ref, ob_ref)):
        outs = []
        for g in range(ATT_KV_HEADS):
            acc = acc_ref[blk * ATT_KV_HEADS + g]
            o_t = acc[:hd, :] / acc[hd:hd + 1, :]
            outs.extend(o_t[:, r * tq:(r + 1) * tq] for r in range(rep))
        o_ref[...] = jnp.concatenate(outs, axis=0).T.astype(o_ref.dtype)


def _dsa(q_t, k, v_t, qi_t, k_idx, wi_t, bsz, seq):
    tq = DSA_QBLOCK
    nq = seq // tq
    half = nq // 2
    top_k = min(TOPK_MAX, seq // 4)
    width = ATT_HEADS * ATT_HEAD_DIM
    gq = (ATT_HEADS // ATT_KV_HEADS) * tq
    col_a = lambda n: pl.BlockSpec((n, tq), lambda b, p: (0, b * nq + p))
    col_b = lambda n: pl.BlockSpec((n, tq), lambda b, p: (0, b * nq + nq - 1 - p))
    full = lambda n: pl.BlockSpec((seq, n), lambda b, p: (b, 0))
    out_a, out_b = pl.pallas_call(
        functools.partial(_dsa_kernel, top_k=top_k, nq=nq),
        out_shape=[jax.ShapeDtypeStruct((bsz, half * tq, width), BF16)] * 2,
        grid=(bsz, half),
        in_specs=[col_a(q_t.shape[0]), col_b(q_t.shape[0]), col_a(qi_t.shape[0]), col_b(qi_t.shape[0]),
                  col_a(wi_t.shape[0]), col_b(wi_t.shape[0]), full(k.shape[1]), full(k_idx.shape[1]),
                  pl.BlockSpec((nq, v_t.shape[1], tq), lambda b, p: (b, 0, 0))],
        out_specs=[pl.BlockSpec((None, tq, width), lambda b, p: (b, p, 0)),
                   pl.BlockSpec((None, tq, width), lambda b, p: (b, half - 1 - p, 0))],
        scratch_shapes=[pltpu.VMEM((nq + 1, tq, tq), F32),
                        pltpu.VMEM((nq + 2, tq, tq), F32),
                        pltpu.VMEM((2 * ATT_KV_HEADS, 1, gq), F32),
                        pltpu.VMEM((2 * ATT_KV_HEADS, ATT_HEAD_DIM + 16, gq), F32),
                        pltpu.VMEM((2 * ATT_KV_HEADS, KAUG_W, gq), BF16),
                        pltpu.VMEM((2, IDX_DIM, IDX_HEADS * tq), BF16),
                        pltpu.VMEM((2, IDX_HEADS, tq), F32),
                        pltpu.VMEM((tq, IDX_HEADS * tq), F32), pltpu.VMEM((tq, IDX_HEADS * tq), F32),
                        pltpu.VMEM((ATT_KV_HEADS, tq, gq), F32), pltpu.VMEM((ATT_KV_HEADS, tq, gq), F32)],
        compiler_params=_params("parallel", "arbitrary"),
        name="dsa",
    )(q_t, q_t, qi_t, qi_t, wi_t, wi_t, k, k_idx, v_t)
    return jnp.concatenate([out_a, out_b], axis=1).reshape(bsz * seq, width)


def _merge_kernel(x_ref, ya_ref, yb_ref, ga_ref, gb_ref, wa_ref, wb_ref, wo_ref, o_ref):
    pa = jnp.dot(ya_ref[...], wa_ref[...], preferred_element_type=F32)
    pb = jnp.dot(yb_ref[...], wb_ref[...], preferred_element_type=F32)
    merged = (jax.nn.sigmoid(ga_ref[...].astype(F32)) * pa + jax.nn.sigmoid(gb_ref[...].astype(F32)) * pb)
    o_ref[...] = x_ref[...] + jnp.dot(merged.astype(BF16), wo_ref[...], preferred_element_type=F32)


def _merge(x1, y_ssm, y_att, g_ssm, g_att, w_a, w_b, w_o):
    t, d = x1.shape
    rows = min(MERGE_ROWS, t)
    row_spec = lambda n: pl.BlockSpec((rows, n), lambda i: (i, 0))
    return pl.pallas_call(
        _merge_kernel,
        out_shape=jax.ShapeDtypeStruct((t, d), F32),
        grid=(t // rows,),
        in_specs=[row_spec(d), row_spec(y_ssm.shape[1]), row_spec(y_att.shape[1]), row_spec(d), row_spec(d),
                  _const_spec(w_a.shape), _const_spec(w_b.shape), _const_spec(w_o.shape)],
        out_specs=row_spec(d),
        compiler_params=_params("parallel"),
        name="merge",
    )(x1, y_ssm, y_att, g_ssm, g_att, w_a, w_b, w_o)


def kernel(x, ffn1_norm, ffn1_w_gate, ffn1_w_up, ffn1_w_down, mix_norm, w_in, conv_w, conv_b, dt_bias, a_log,
           d_skip, ssm_norm, w_branch_ssm, w_branch_attn, w_out, ffn2_norm, ffn2_w_gate, ffn2_w_up, ffn2_w_down,
           final_norm):
    bsz, seq, d = x.shape
    depth = ffn1_norm.shape[0]
    h = x.reshape(bsz * seq, d)
    for l in range(depth):
        last = l == depth - 1
        h = _ffn(h, ffn1_norm[l], ffn1_w_gate[l].astype(BF16), ffn1_w_up[l].astype(BF16),
                 ffn1_w_down[l].astype(BF16))
        z, xbc, k, g_ssm, g_att, dt_raw, k_idx, q_t, qi_t, v_t, wi_t = _in_proj(h, mix_norm[l], w_in[l], seq)
        y_ssm = _ssd(xbc, z, dt_raw, conv_w[l], conv_b[l], dt_bias[l], a_log[l], d_skip[l], ssm_norm[l], bsz, seq)
        y_att = _dsa(q_t, k, v_t, qi_t, k_idx, wi_t, bsz, seq)
        h = _merge(h, y_ssm, y_att, g_ssm, g_att, w_branch_ssm[l].astype(BF16), w_branch_attn[l].astype(BF16),
                   w_out[l].astype(BF16))
        h = _ffn(h, ffn2_norm[l], ffn2_w_gate[l].astype(BF16), ffn2_w_up[l].astype(BF16),
                 ffn2_w_down[l].astype(BF16), final_g=final_norm if last else None)
    return h.reshape(bsz, seq, d)
```

```python
import functools

import jax
import jax.numpy as jnp
import numpy as np
from jax import lax
from jax.experimental import pallas as pl
from jax.experimental.pallas import tpu as pltpu

F32 = jnp.float32
BF16 = jnp.bfloat16

D_MODEL = 1024
D_INNER = 2048
SSM_HEAD_DIM = 64
SSM_HEADS = 32
SSM_GROUPS = 4
SSM_STATE = 128
CONV_K = 4
XBC_W = D_INNER + 2 * SSM_GROUPS * SSM_STATE
ATT_HEADS = 16
ATT_HEAD_DIM = 64
ATT_KV_HEADS = 4
IDX_HEADS = 8
IDX_DIM = 64
TOPK_MAX = 256
ALIBI_BASE = 8.0
D_FF = 2816
FFN_RESIDUAL_SCALE = 0.5
EPS = 1e-6

IN_SIZES = (D_INNER, XBC_W, SSM_HEADS, ATT_HEADS * ATT_HEAD_DIM, ATT_KV_HEADS * ATT_HEAD_DIM,
            ATT_KV_HEADS * ATT_HEAD_DIM, IDX_HEADS * IDX_DIM, IDX_DIM, IDX_HEADS, D_MODEL, D_MODEL)

VMEM_LIMIT_BYTES = 56 * 1024 * 1024
LOG2E = 1.4426950408889634
SSD_CHUNK = 128
CONV_HALO = 16
SSD_CONV_COLS = 512
DSA_QBLOCK = 128
KAUG_W = 128
FFN_ROWS = 512
FFN_FCHUNK = 256
PROJ_ROWS = 512
MERGE_ROWS = 512
MASK_NEG = -1e30


def _rms(x, g):
    ms = jnp.mean(x * x, axis=-1, keepdims=True)
    return x * lax.rsqrt(ms + EPS) * g


def _silu(x):
    return x / (1.0 + jnp.exp2(x * -LOG2E))


def _const_spec(shape):
    return pl.BlockSpec(shape, lambda *_: (0,) * len(shape), pipeline_mode=pl.Buffered(1))


def _params(*sem):
    return pltpu.CompilerParams(dimension_semantics=sem, vmem_limit_bytes=VMEM_LIMIT_BYTES)


def _ffn_kernel(x_ref, g_ref, wg_ref, wu_ref, wd_ref, *rest, final_norm):
    if final_norm:
        fg_ref, o_ref, acc_ref = rest
    else:
        o_ref, acc_ref = rest
    x = x_ref[...]
    xn = _rms(x, g_ref[...]).astype(BF16)
    d_ff = wg_ref.shape[1]
    for c in range(0, d_ff, FFN_FCHUNK):
        gate = jnp.dot(xn, wg_ref[:, c:c + FFN_FCHUNK], preferred_element_type=F32)
        up = jnp.dot(xn, wu_ref[:, c:c + FFN_FCHUNK], preferred_element_type=F32)
        act = (gate * jax.nn.sigmoid(gate) * up).astype(BF16)
        part = jnp.dot(act, wd_ref[c:c + FFN_FCHUNK, :], preferred_element_type=F32)
        if c == 0:
            acc_ref[...] = part
        else:
            acc_ref[...] += part
    y = x + FFN_RESIDUAL_SCALE * acc_ref[...]
    if final_norm:
        y = _rms(y, fg_ref[...])
    o_ref[...] = y


def _ffn(x, norm_g, wg, wu, wd, final_g=None):
    t, d = x.shape
    d_ff = wg.shape[1]
    rows = min(FFN_ROWS, t)
    in_specs = [pl.BlockSpec((rows, d), lambda i: (i, 0)), _const_spec((1, d)),
                _const_spec((d, d_ff)), _const_spec((d, d_ff)), _const_spec((d_ff, d))]
    args = [x, norm_g.reshape(1, d), wg, wu, wd]
    if final_g is not None:
        in_specs.append(_const_spec((1, d)))
        args.append(final_g.reshape(1, d))
    return pl.pallas_call(
        functools.partial(_ffn_kernel, final_norm=final_g is not None),
        out_shape=jax.ShapeDtypeStruct((t, d), F32),
        grid=(t // rows,),
        in_specs=in_specs,
        out_specs=pl.BlockSpec((rows, d), lambda i: (i, 0)),
        scratch_shapes=[pltpu.VMEM((rows, d), F32)],
        compiler_params=_params("parallel"),
        name="ffn_final" if final_g is not None else "ffn",
    )(*args)


def _dot_nt(a, b):
    return lax.dot_general(a, b, (((1,), (1,)), ((), ())), preferred_element_type=F32)


def _in_proj_kernel(x_ref, g_ref, wz, wxbc, wk, wgs, wga, wmisc, wqt, wqit, wvt, wwit,
                    oz, oxbc, ok, ogs, oga, odt, oki, oqt, oqit, ovt, owit, *, seq):
    xn = _rms(x_ref[...], g_ref[...]).astype(BF16)
    for w_ref, o_ref in ((wz, oz), (wxbc, oxbc), (wgs, ogs), (wga, oga)):
        n = w_ref.shape[1]
        step = min(n, 512)
        for c in range(0, n, step):
            o_ref[:, c:c + step] = jnp.dot(xn, w_ref[:, c:c + step], preferred_element_type=F32).astype(o_ref.dtype)
    rows = x_ref.shape[0]
    pos = (pl.program_id(0) * rows) % seq + lax.broadcasted_iota(jnp.int32, (rows, KAUG_W), 0)
    lane = lax.broadcasted_iota(jnp.int32, (rows, KAUG_W), 1) - ATT_HEAD_DIM
    chunk_f = lax.shift_right_logical(pos, DSA_QBLOCK.bit_length() - 1).astype(F32)
    row_f = (pos & (DSA_QBLOCK - 1)).astype(F32)
    pos_cols = jnp.where((lane >= 0) & (lane < 3), chunk_f,
                         jnp.where((lane >= 3) & (lane < 6), row_f,
                                   jnp.where((lane >= 6) & (lane < 9), 1.0, 0.0)))
    for g in range(ATT_KV_HEADS):
        cols = slice(g * KAUG_W, (g + 1) * KAUG_W)
        ok[:, cols] = (jnp.dot(xn, wk[:, cols], preferred_element_type=F32) * LOG2E + pos_cols).astype(BF16)
    misc = jnp.dot(xn, wmisc[...], preferred_element_type=F32)
    odt[...] = misc[:, :SSM_HEADS]
    oki[...] = misc[:, SSM_HEADS:SSM_HEADS + IDX_DIM].astype(BF16)
    for w_ref, o_ref in ((wqt, oqt), (wqit, oqit)):
        n = w_ref.shape[0]
        for c in range(0, n, 256):
            o_ref[c:c + 256, :] = _dot_nt(w_ref[c:c + 256, :], xn).astype(o_ref.dtype)
    owit[...] = _dot_nt(wwit[...], xn)[:IDX_HEADS, :]
    vt = _dot_nt(wvt[...], xn).astype(BF16)
    for j in range(ovt.shape[0]):
        ovt[j] = vt[:, j * DSA_QBLOCK:(j + 1) * DSA_QBLOCK]


def _in_proj(x1, norm_g, w_in, seq):
    t, d = x1.shape
    offs = [0]
    for s in IN_SIZES:
        offs.append(offs[-1] + s)
    seg = lambda i: w_in[:, offs[i]:offs[i + 1]]
    w_misc = jnp.concatenate([seg(2), seg(7), jnp.zeros((d, 128 - SSM_HEADS - IDX_DIM), w_in.dtype)], axis=1)
    w_k = seg(4).reshape(d, ATT_KV_HEADS, ATT_HEAD_DIM)
    w_k = jnp.pad(w_k, ((0, 0), (0, 0), (0, KAUG_W - ATT_HEAD_DIM))).reshape(d, ATT_KV_HEADS * KAUG_W)
    natural = [seg(0), seg(1), w_k, seg(9), seg(10), w_misc]
    w_idx_t = jnp.concatenate([seg(8).T, jnp.zeros((16 - IDX_HEADS, d), w_in.dtype)], axis=0)
    transposed = [seg(3).T, seg(6).T, seg(5).T, w_idx_t]
    weights = [w.astype(BF16) for w in natural + transposed]
    rows = min(PROJ_ROWS, t)
    nchunk = rows // DSA_QBLOCK
    row_spec = lambda n: pl.BlockSpec((rows, n), lambda i: (i, 0))
    col_spec = lambda n: pl.BlockSpec((n, rows), lambda i: (0, i))
    kv_w = ATT_KV_HEADS * ATT_HEAD_DIM
    out_shapes = [jax.ShapeDtypeStruct((t, w.shape[1]), BF16) for w in natural[:-1]]
    out_shapes += [jax.ShapeDtypeStruct((t, SSM_HEADS), F32), jax.ShapeDtypeStruct((t, IDX_DIM), BF16),
                   jax.ShapeDtypeStruct((ATT_HEADS * ATT_HEAD_DIM, t), BF16),
                   jax.ShapeDtypeStruct((IDX_HEADS * IDX_DIM, t), BF16),
                   jax.ShapeDtypeStruct((t // DSA_QBLOCK, kv_w, DSA_QBLOCK), BF16),
                   jax.ShapeDtypeStruct((IDX_HEADS, t), F32)]
    out_specs = [row_spec(s.shape[1]) for s in out_shapes[:7]]
    out_specs += [col_spec(ATT_HEADS * ATT_HEAD_DIM), col_spec(IDX_HEADS * IDX_DIM),
                  pl.BlockSpec((nchunk, kv_w, DSA_QBLOCK), lambda i: (i, 0, 0)), col_spec(IDX_HEADS)]
    return pl.pallas_call(
        functools.partial(_in_proj_kernel, seq=seq),
        out_shape=out_shapes,
        grid=(t // rows,),
        in_specs=[row_spec(d), _const_spec((1, d))] + [_const_spec(w.shape) for w in weights],
        out_specs=out_specs,
        compiler_params=_params("parallel"),
        name="in_proj",
    )(x1, norm_g.reshape(1, d), *weights)


def _ssd_kernel(xbc_ref, z_ref, dt_ref, shift_ref, convw_ref, convb_ref, dtb_ref, ahead2_ref, dexp_ref, normw_ref,
                o_ref, win_ref, state_ref):
    q = SSD_CHUNK
    hd, n, g = SSM_HEAD_DIM, SSM_STATE, SSM_GROUPS
    heads = SSM_HEADS
    hpg = heads // g
    gw = hpg * hd
    c = pl.program_id(1)

    @pl.when(c == 0)
    def _():
        win_ref[0:CONV_HALO, :] = jnp.zeros((CONV_HALO, XBC_W), BF16)
        state_ref[...] = jnp.zeros_like(state_ref)

    x_cur = xbc_ref[...]
    win_ref[CONV_HALO:CONV_HALO + q, :] = x_cur
    pieces = []
    for c0 in range(0, XBC_W, SSD_CONV_COLS):
        cols = slice(c0, c0 + SSD_CONV_COLS)
        shifted = jnp.dot(shift_ref[...], win_ref[:, cols], preferred_element_type=F32)
        conv = convb_ref[:, cols] + convw_ref[CONV_K - 1:CONV_K, cols] * x_cur[:, cols].astype(F32)
        for k in range(CONV_K - 1):
            conv = conv + convw_ref[k:k + 1, cols] * shifted[k * q:(k + 1) * q, :]
        pieces.append(_silu(conv))
    win_ref[0:CONV_HALO, :] = x_cur[q - CONV_HALO:, :]
    xbc = jnp.concatenate(pieces, axis=1)
    xs = xbc[:, :D_INNER]
    bm = xbc[:, D_INNER:D_INNER + g * n]
    cm = xbc[:, D_INNER + g * n:]

    dt = jax.nn.softplus(dt_ref[...] + dtb_ref[...])
    a2 = dt * ahead2_ref[...]
    row = lax.broadcasted_iota(jnp.int32, (q, q), 0)
    col = lax.broadcasted_iota(jnp.int32, (q, q), 1)
    causal = row >= col
    tril = jnp.where(causal, 1.0, 0.0).astype(BF16)
    a2_cs = _split_dot_left(tril, a2)
    a2_cs_t = _transpose_heads(a2_cs)
    a2_last = a2_cs[q - 1:q, :]
    stack = jnp.concatenate([dt, jnp.exp2(a2_cs), dt * jnp.exp2(a2_last - a2_cs),
                             jnp.broadcast_to(jnp.exp2(a2_last), (16, heads))], axis=0)
    hi = stack.astype(BF16)
    lo = (stack - hi.astype(F32)).astype(BF16)
    lane = lax.broadcasted_iota(jnp.int32, (2 * heads, heads * hd), 1)
    src = lax.broadcasted_iota(jnp.int32, (2 * heads, heads * hd), 0)
    first = jnp.where(src >= heads, src - heads, src) * hd
    expand = jnp.where((lane >= first) & (lane < first + hd), 1.0, 0.0).astype(BF16)
    ex = jnp.dot(jnp.concatenate([hi, lo], axis=1), expand, preferred_element_type=F32)
    dt_x, in_decay_x, w_state_x, chunk_decay_x = ex[0:q], ex[q:2 * q], ex[2 * q:3 * q], ex[3 * q:3 * q + 1]

    prev = state_ref[...]
    prev_b = prev.astype(BF16)
    lane_pair = lax.broadcasted_iota(jnp.int32, (q, 2 * hd), 1)
    ys = []
    for gi in range(g):
        sl = slice(gi * gw, (gi + 1) * gw)
        bg = bm[:, gi * n:(gi + 1) * n]
        bg_b = bg.astype(BF16)
        cg_b = cm[:, gi * n:(gi + 1) * n].astype(BF16)
        cb = jnp.where(causal, _dot_nt(cg_b, bg_b), 0.0)
        xs_g = xs[:, sl]
        xdt_b = (xs_g * dt_x[:, sl]).astype(BF16)
        y_off = jnp.dot(cg_b, prev_b[:, sl], preferred_element_type=F32)
        pairs = []
        for pr in range(hpg // 2):
            ms = []
            for h in (gi * hpg + 2 * pr, gi * hpg + 2 * pr + 1):
                seg = jnp.broadcast_to(a2_cs[:, h:h + 1], (q, q)) - a2_cs_t[h:h + 1, :]
                ms.append((cb * jnp.exp2(jnp.minimum(seg, 0.0))).astype(BF16))
            xp = xdt_b[:, 2 * pr * hd:(2 * pr + 2) * hd]
            zero = jnp.zeros_like(xp)
            rhs = jnp.concatenate([jnp.where(lane_pair < hd, xp, zero), jnp.where(lane_pair >= hd, xp, zero)], axis=0)
            pairs.append(jnp.dot(jnp.concatenate(ms, axis=1), rhs, preferred_element_type=F32))
        ys.append(jnp.concatenate(pairs, axis=1) + y_off * in_decay_x[:, sl] + dexp_ref[:, sl] * xs_g)
        xw = (xs_g * w_state_x[:, sl]).astype(BF16)
        s_new = jnp.dot(bg.T.astype(BF16), xw, preferred_element_type=F32)
        state_ref[:, sl] = prev[:, sl] * chunk_decay_x[:, sl] + s_new
    y = jnp.concatenate(ys, axis=1)

    z = z_ref[...].astype(F32)
    yg = y * _silu(z)
    outs = []
    for gi in range(g):
        blk = yg[:, gi * gw:(gi + 1) * gw]
        ms = jnp.mean(blk * blk, axis=-1, keepdims=True)
        outs.append(blk * lax.rsqrt(ms + EPS))
    o_ref[...] = (jnp.concatenate(outs, axis=1) * normw_ref[...]).astype(o_ref.dtype)


def _transpose_heads(x):
    q, h = x.shape
    return jnp.concatenate([x, jnp.zeros((q, q - h), x.dtype)], axis=1).T[:h, :]


def _split_dot_left(b01, a):
    out = None
    rem = a
    for _ in range(3):
        hi = rem.astype(BF16)
        term = jnp.dot(b01, hi, preferred_element_type=F32)
        out = term if out is None else out + term
        rem = rem - hi.astype(F32)
    return out


def _conv_shift_matrix(q):
    s = np.zeros(((CONV_K - 1) * q, CONV_HALO + q), np.float32)
    for k in range(CONV_K - 1):
        s[k * q + np.arange(q), CONV_HALO + np.arange(q) - (CONV_K - 1) + k] = 1.0
    return jnp.asarray(s, BF16)


def _ssd(xbc, z, dt_raw, conv_w, conv_b, dt_bias, a_log, d_skip, norm_w, bsz, seq):
    q = SSD_CHUNK
    nc = seq // q
    a_head2 = (-jnp.exp(a_log.astype(F32)) * LOG2E).reshape(1, SSM_HEADS)
    d_exp = jnp.repeat(d_skip.astype(F32), SSM_HEAD_DIM).reshape(1, D_INNER)
    shift = _conv_shift_matrix(q)
    row_spec = lambda n: pl.BlockSpec((q, n), lambda b, c: (b * nc + c, 0))
    return pl.pallas_call(
        _ssd_kernel,
        out_shape=jax.ShapeDtypeStruct((bsz * seq, D_INNER), BF16),
        grid=(bsz, nc),
        in_specs=[row_spec(XBC_W), row_spec(D_INNER), row_spec(SSM_HEADS), _const_spec(shift.shape),
                  _const_spec((CONV_K, XBC_W)), _const_spec((1, XBC_W)), _const_spec((1, SSM_HEADS)),
                  _const_spec((1, SSM_HEADS)), _const_spec((1, D_INNER)), _const_spec((1, D_INNER))],
        out_specs=row_spec(D_INNER),
        scratch_shapes=[pltpu.VMEM((CONV_HALO + q, XBC_W), BF16), pltpu.VMEM((SSM_STATE, D_INNER), F32)],
        compiler_params=_params("parallel", "arbitrary"),
        name="ssd",
    )(xbc, z, dt_raw, shift, conv_w.astype(F32), conv_b.reshape(1, XBC_W).astype(F32),
      dt_bias.reshape(1, SSM_HEADS).astype(F32), a_head2, d_exp, norm_w.reshape(1, D_INNER).astype(F32))


def _sortable_to_f32(s):
    bits = jnp.where(s < 0, s ^ jnp.int32(-2 ** 31), ~s)
    return lax.bitcast_convert_type(bits, F32)


def _tile_count(mask):
    ones = jnp.where(mask, 1.0, 0.0)
    return jnp.sum(ones.reshape(ones.shape[0] // 8, 8, ones.shape[1]), axis=0)


def _np_bf16_pieces(x, n=3):
    out = []
    rem = np.float32(x)
    for _ in range(n):
        piece = np.float32(np.asarray(rem, dtype=jnp.bfloat16))
        out.append(piece)
        rem = np.float32(rem - piece)
    return out


def _bf16_pieces(x, n=3):
    out = []
    rem = x
    for _ in range(n):
        piece = rem.astype(BF16).astype(F32)
        out.append(piece)
        rem = rem - piece
    return out


def _dsa_kernel(qa_ref, qb_ref, qia_ref, qib_ref, wa_ref, wb_ref, k_ref, ki_ref, vt_ref, oa_ref, ob_ref,
                sc_ref, mask_ref, m_ref, acc_ref, qaug_ref, qis_ref, ws_ref, rel0_ref, rel1_ref, raw0_ref, raw1_ref,
                *, top_k, nq):
    tq = DSA_QBLOCK
    rep = ATT_HEADS // ATT_KV_HEADS
    hd = ATT_HEAD_DIM
    p = pl.program_id(1)
    n_a = p + 1
    nslot = nq + 1
    t_a = p * tq + lax.broadcasted_iota(jnp.int32, (1, tq), 1)
    t_b = (nq - 1 - p) * tq + lax.broadcasted_iota(jnp.int32, (1, tq), 1)
    key_in_tile = lax.broadcasted_iota(jnp.int32, (tq, tq), 0)
    kf = float(top_k)

    def slot_info(s):
        is_a = s <= p
        return jnp.where(is_a, 0, 1), jnp.minimum(jnp.where(is_a, s, s - n_a), nq - 1)

    for blk, (qi_ref, w_ref, q_ref, t_row) in enumerate(((qia_ref, wa_ref, qa_ref, t_a), (qib_ref, wb_ref, qb_ref, t_b))):
        qi = qi_ref[...]
        qis_ref[blk] = jnp.concatenate([qi[j * IDX_DIM:(j + 1) * IDX_DIM, :] for j in range(IDX_HEADS)], axis=1)
        ws_ref[blk] = w_ref[...] * (IDX_HEADS ** -0.5) * (IDX_DIM ** -0.5)
        row16 = lax.broadcasted_iota(jnp.int32, (16, tq), 0)
        for g in range(ATT_KV_HEADS):
            q_t = jnp.concatenate([q_ref[(g * rep + r) * hd:(g * rep + r + 1) * hd, :] for r in range(rep)], axis=1)
            bias_rows = []
            for r in range(rep):
                slope = np.float32(2.0 ** (-ALIBI_BASE * (g * rep + r + 1) / ATT_HEADS) * LOG2E)
                pieces = [float(v) for v in _np_bf16_pieces(slope)]
                terms = [tq * v for v in pieces] + pieces + _bf16_pieces(-(slope * t_row.astype(F32)))
                rows = jnp.zeros((16, tq), F32)
                for i, term in enumerate(terms):
                    rows = jnp.where(row16 == i, term, rows)
                bias_rows.append(rows)
            qaug_ref[blk * ATT_KV_HEADS + g] = jnp.concatenate(
                [q_t * (hd ** -0.5), jnp.concatenate(bias_rows, axis=1).astype(BF16),
                 jnp.zeros((KAUG_W - hd - 16, rep * tq), BF16)], axis=0)

    def issue_rel(s, rel_ref):
        blk, c = slot_info(s)
        kc = ki_ref[pl.ds(pl.multiple_of(c * tq, tq), tq), :]
        rel_ref[...] = jnp.dot(kc, qis_ref[blk], preferred_element_type=F32)

    def score_slot(s, rel_ref):
        blk, c = slot_info(s)
        w = ws_ref[blk]
        acc = w[0:1, :] * jnp.maximum(rel_ref[:, 0:tq], 0.0)
        for j in range(1, IDX_HEADS):
            acc = acc + w[j:j + 1, :] * jnp.maximum(rel_ref[:, j * tq:(j + 1) * tq], 0.0)
        sc_ref[s] = jnp.where(c * tq + key_in_tile <= jnp.where(blk == 0, t_a, t_b), acc, -jnp.inf)

    issue_rel(0, rel0_ref)

    def score_trip(j, carry):
        s0 = 2 * j
        s1 = jnp.minimum(s0 + 1, nslot - 1)
        issue_rel(s1, rel1_ref)
        score_slot(s0, rel0_ref)
        issue_rel(jnp.minimum(s0 + 2, nslot - 1), rel0_ref)
        score_slot(s1, rel1_ref)
        return carry

    lax.fori_loop(0, (nslot + 1) // 2, score_trip, 0, unroll=True)

    def both_counts(pred):
        acc_a = jnp.zeros((8, tq), F32)
        acc_b = jnp.zeros((8, tq), F32)
        for s in range(nslot):
            is_a = s <= p
            part = _tile_count(pred(s, sc_ref[s], is_a))
            if s == 0:
                acc_a = acc_a + part
            elif s >= nq // 2:
                acc_b = acc_b + part
            else:
                acc_a = acc_a + jnp.where(is_a, part, 0.0)
                acc_b = acc_b + jnp.where(is_a, 0.0, part)
        return jnp.sum(acc_a, axis=0, keepdims=True), jnp.sum(acc_b, axis=0, keepdims=True)

    def value_step(b, carry):
        pre_a, pre_b = carry
        bit = lax.shift_left(jnp.int32(1), 31 - b)
        cand_a, cand_b = pre_a | bit, pre_b | bit
        cf_a, cf_b = _sortable_to_f32(cand_a), _sortable_to_f32(cand_b)
        cnt_a, cnt_b = both_counts(lambda s, tile, is_a: tile >= jnp.where(is_a, cf_a, cf_b))
        return jnp.where(cnt_a >= kf, cand_a, pre_a), jnp.where(cnt_b >= kf, cand_b, pre_b)

    zero_row = jnp.zeros((1, tq), jnp.int32)
    pre_a, pre_b = lax.fori_loop(0, 32, value_step, (zero_row, zero_row))
    thr_a, thr_b = _sortable_to_f32(pre_a), _sortable_to_f32(pre_b)
    above_a, above_b = both_counts(lambda s, tile, is_a: tile > jnp.where(is_a, thr_a, thr_b))
    need_a, need_b = kf - above_a, kf - above_b

    def slot_key(s, is_a):
        return jnp.where(is_a, s, s - n_a) * tq + key_in_tile

    def index_step(b, carry):
        lo_a, lo_b = carry
        bit = lax.shift_left(jnp.int32(1), b)
        cand_a, cand_b = lo_a + bit, lo_b + bit
        cnt_a, cnt_b = both_counts(
            lambda s, tile, is_a: (tile == jnp.where(is_a, thr_a, thr_b))
            & (slot_key(s, is_a) < jnp.where(is_a, cand_a, cand_b)))
        return jnp.where(cnt_a < need_a, cand_a, lo_a), jnp.where(cnt_b < need_b, cand_b, lo_b)

    tie_a, tie_b = both_counts(lambda s, tile, is_a: tile == jnp.where(is_a, thr_a, thr_b))
    excess = jnp.maximum(jnp.max(tie_a - need_a), jnp.max(tie_b - need_b))
    nbits = (nq * tq - 1).bit_length()
    all_keys = jnp.full((1, tq), nq * tq, jnp.int32)
    last_a, last_b = lax.cond(
        excess > 0.0,
        lambda: lax.fori_loop(0, nbits, lambda b, c: index_step(nbits - 1 - b, c), (zero_row, zero_row)),
        lambda: (all_keys, all_keys))

    def finish(t_row, thr, last):
        early = t_row < top_k
        return jnp.where(early, -jnp.finfo(F32).max, thr), jnp.where(early, nq * tq, last)

    thr_a, last_a = finish(t_a, thr_a, last_a)
    thr_b, last_b = finish(t_b, thr_b, last_b)

    n_b = nq - p
    a_slots = n_a + (n_a & 1)

    def att_slot_info(s):
        is_a = s < a_slots
        local = jnp.where(is_a, s, s - a_slots)
        count = jnp.where(is_a, n_a, n_b)
        chunk = jnp.minimum(local, count - 1)
        return jnp.where(is_a, 0, 1), chunk, jnp.where(is_a, chunk, n_a + chunk), local >= count

    for s in range(nslot + 1):
        blk, chunk, src, spare = att_slot_info(s)
        is_a = blk == 0
        thr = jnp.where(is_a, thr_a, thr_b)
        tile = sc_ref[src]
        key = chunk * tq + key_in_tile
        sel = (tile > thr) | ((tile == thr) & (key <= jnp.where(is_a, last_a, last_b)))
        mask_ref[s] = jnp.where(sel & jnp.logical_not(spare), 0.0, MASK_NEG)

    ones_rows = jnp.where(lax.broadcasted_iota(jnp.int32, (16, 2 * tq), 0) == 0, 1.0, 0.0).astype(BF16)
    m_ref[...] = jnp.full(m_ref.shape, -jnp.inf, F32)
    acc_ref[...] = jnp.zeros(acc_ref.shape, F32)

    def issue_qk(pair, raw_ref):
        for u in range(2):
            blk, c, _, _ = att_slot_info(2 * pair + u)
            kc = k_ref[pl.ds(pl.multiple_of(c * tq, tq), tq), :]
            for g in range(ATT_KV_HEADS):
                raw_ref[u, g] = jnp.dot(kc[:, g * KAUG_W:(g + 1) * KAUG_W], qaug_ref[blk * ATT_KV_HEADS + g],
                                        preferred_element_type=F32)

    def attend_pair(pair, raw_ref):
        blk, c0, _, _ = att_slot_info(2 * pair)
        _, c1, _, _ = att_slot_info(2 * pair + 1)
        masks = (mask_ref[2 * pair], mask_ref[2 * pair + 1])
        for g in range(ATT_KV_HEADS):
            slot = blk * ATT_KV_HEADS + g
            m_old = m_ref[slot]
            logits = jnp.concatenate([raw_ref[0, g, :, r * tq:(r + 1) * tq] + masks[0] for r in range(rep)], axis=1)
            m_mid = jnp.maximum(m_old, jnp.max(logits, axis=0, keepdims=True))
            prob0 = jnp.exp2(logits - m_mid).astype(BF16)
            logits = jnp.concatenate([raw_ref[1, g, :, r * tq:(r + 1) * tq] + masks[1] for r in range(rep)], axis=1)
            m_new = jnp.maximum(m_mid, jnp.max(logits, axis=0, keepdims=True))
            prob1 = jnp.exp2(logits - m_new).astype(BF16)
            prob = jnp.concatenate([prob0 * jnp.exp2(m_mid - m_new).astype(BF16), prob1], axis=0)
            v_pair = jnp.concatenate([vt_ref[c0, g * hd:(g + 1) * hd, :], vt_ref[c1, g * hd:(g + 1) * hd, :]], axis=1)
            v_aug = jnp.concatenate([v_pair, ones_rows], axis=0)
            pv = jnp.dot(v_aug, prob, preferred_element_type=F32)
            acc_ref[slot] = acc_ref[slot] * jnp.exp2(m_old - m_new) + pv
            m_ref[slot] = m_new

    npair = (nslot + 1) // 2
    issue_qk(0, raw0_ref)
    for j in range(npair):
        cur, nxt = (raw0_ref, raw1_ref) if j % 2 == 0 else (raw1_ref, raw0_ref)
        if j + 1 < npair:
            issue_qk(j + 1, nxt)
        attend_pair(j, cur)

    for blk, o_ref in enumerate((oa_ref, ob_ref)):
        outs = []
        for g in range(ATT_KV_HEADS):
            acc = acc_ref[blk * ATT_KV_HEADS + g]
            o_t = acc[:hd, :] / acc[hd:hd + 1, :]
            outs.extend(o_t[:, r * tq:(r + 1) * tq] for r in range(rep))
        o_ref[...] = jnp.concatenate(outs, axis=0).T.astype(o_ref.dtype)


def _dsa(q_t, k, v_t, qi_t, k_idx, wi_t, bsz, seq):
    tq = DSA_QBLOCK
    nq = seq // tq
    half = nq // 2
    top_k = min(TOPK_MAX, seq // 4)
    width = ATT_HEADS * ATT_HEAD_DIM
    gq = (ATT_HEADS // ATT_KV_HEADS) * tq
    col_a = lambda n: pl.BlockSpec((n, tq), lambda b, p: (0, b * nq + p))
    col_b = lambda n: pl.BlockSpec((n, tq), lambda b, p: (0, b * nq + nq - 1 - p))
    full = lambda n: pl.BlockSpec((seq, n), lambda b, p: (b, 0))
    out_a, out_b = pl.pallas_call(
        functools.partial(_dsa_kernel, top_k=top_k, nq=nq),
        out_shape=[jax.ShapeDtypeStruct((bsz, half * tq, width), BF16)] * 2,
        grid=(bsz, half),
        in_specs=[col_a(q_t.shape[0]), col_b(q_t.shape[0]), col_a(qi_t.shape[0]), col_b(qi_t.shape[0]),
                  col_a(wi_t.shape[0]), col_b(wi_t.shape[0]), full(k.shape[1]), full(k_idx.shape[1]),
                  pl.BlockSpec((nq, v_t.shape[1], tq), lambda b, p: (b, 0, 0))],
        out_specs=[pl.BlockSpec((None, tq, width), lambda b, p: (b, p, 0)),
                   pl.BlockSpec((None, tq, width), lambda b, p: (b, half - 1 - p, 0))],
        scratch_shapes=[pltpu.VMEM((nq + 1, tq, tq), F32),
                        pltpu.VMEM((nq + 2, tq, tq), F32),
                        pltpu.VMEM((2 * ATT_KV_HEADS, 1, gq), F32),
                        pltpu.VMEM((2 * ATT_KV_HEADS, ATT_HEAD_DIM + 16, gq), F32),
                        pltpu.VMEM((2 * ATT_KV_HEADS, KAUG_W, gq), BF16),
                        pltpu.VMEM((2, IDX_DIM, IDX_HEADS * tq), BF16),
                        pltpu.VMEM((2, IDX_HEADS, tq), F32),
                        pltpu.VMEM((tq, IDX_HEADS * tq), F32), pltpu.VMEM((tq, IDX_HEADS * tq), F32),
                        pltpu.VMEM((2, ATT_KV_HEADS, tq, gq), F32),
                        pltpu.VMEM((2, ATT_KV_HEADS, tq, gq), F32)],
        compiler_params=_params("parallel", "arbitrary"),
        name="dsa",
    )(q_t, q_t, qi_t, qi_t, wi_t, wi_t, k, k_idx, v_t)
    return jnp.concatenate([out_a, out_b], axis=1).reshape(bsz * seq, width)


def _merge_kernel(x_ref, ya_ref, yb_ref, ga_ref, gb_ref, wa_ref, wb_ref, wo_ref, o_ref):
    pa = jnp.dot(ya_ref[...], wa_ref[...], preferred_element_type=F32)
    pb = jnp.dot(yb_ref[...], wb_ref[...], preferred_element_type=F32)
    merged = (jax.nn.sigmoid(ga_ref[...].astype(F32)) * pa + jax.nn.sigmoid(gb_ref[...].astype(F32)) * pb)
    o_ref[...] = x_ref[...] + jnp.dot(merged.astype(BF16), wo_ref[...], preferred_element_type=F32)


def _merge(x1, y_ssm, y_att, g_ssm, g_att, w_a, w_b, w_o):
    t, d = x1.shape
    rows = min(MERGE_ROWS, t)
    row_spec = lambda n: pl.BlockSpec((rows, n), lambda i: (i, 0))
    return pl.pallas_call(
        _merge_kernel,
        out_shape=jax.ShapeDtypeStruct((t, d), F32),
        grid=(t // rows,),
        in_specs=[row_spec(d), row_spec(y_ssm.shape[1]), row_spec(y_att.shape[1]), row_spec(d), row_spec(d),
                  _const_spec(w_a.shape), _const_spec(w_b.shape), _const_spec(w_o.shape)],
        out_specs=row_spec(d),
        compiler_params=_params("parallel"),
        name="merge",
    )(x1, y_ssm, y_att, g_ssm, g_att, w_a, w_b, w_o)


def kernel(x, ffn1_norm, ffn1_w_gate, ffn1_w_up, ffn1_w_down, mix_norm, w_in, conv_w, conv_b, dt_bias, a_log,
           d_skip, ssm_norm, w_branch_ssm, w_branch_attn, w_out, ffn2_norm, ffn2_w_gate, ffn2_w_up, ffn2_w_down,
           final_norm):
    bsz, seq, d = x.shape
    depth = ffn1_norm.shape[0]
    h = x.reshape(bsz * seq, d)
    for l in range(depth):
        last = l == depth - 1
        h = _ffn(h, ffn1_norm[l], ffn1_w_gate[l].astype(BF16), ffn1_w_up[l].astype(BF16),
                 ffn1_w_down[l].astype(BF16))
        z, xbc, k, g_ssm, g_att, dt_raw, k_idx, q_t, qi_t, v_t, wi_t = _in_proj(h, mix_norm[l], w_in[l], seq)
        y_ssm = _ssd(xbc, z, dt_raw, conv_w[l], conv_b[l], dt_bias[l], a_log[l], d_skip[l], ssm_norm[l], bsz, seq)
        y_att = _dsa(q_t, k, v_t, qi_t, k_idx, wi_t, bsz, seq)
        h = _merge(h, y_ssm, y_att, g_ssm, g_att, w_branch_ssm[l].astype(BF16), w_branch_attn[l].astype(BF16),
                   w_out[l].astype(BF16))
        h = _ffn(h, ffn2_norm[l], ffn2_w_gate[l].astype(BF16), ffn2_w_up[l].astype(BF16),
                 ffn2_w_down[l].astype(BF16), final_g=final_norm if last else None)
    return h.reshape(bsz, seq, d)
```

```python
import functools

import jax
import jax.numpy as jnp
import numpy as np
from jax import lax
from jax.experimental import pallas as pl
from jax.experimental.pallas import tpu as pltpu

F32 = jnp.float32
BF16 = jnp.bfloat16

D_MODEL = 1024
D_INNER = 2048
SSM_HEAD_DIM = 64
SSM_HEADS = 32
SSM_GROUPS = 4
SSM_STATE = 128
CONV_K = 4
XBC_W = D_INNER + 2 * SSM_GROUPS * SSM_STATE
ATT_HEADS = 16
ATT_HEAD_DIM = 64
ATT_KV_HEADS = 4
IDX_HEADS = 8
IDX_DIM = 64
TOPK_MAX = 256
ALIBI_BASE = 8.0
D_FF = 2816
FFN_RESIDUAL_SCALE = 0.5
EPS = 1e-6

IN_SIZES = (D_INNER, XBC_W, SSM_HEADS, ATT_HEADS * ATT_HEAD_DIM, ATT_KV_HEADS * ATT_HEAD_DIM,
            ATT_KV_HEADS * ATT_HEAD_DIM, IDX_HEADS * IDX_DIM, IDX_DIM, IDX_HEADS, D_MODEL, D_MODEL)

VMEM_LIMIT_BYTES = 56 * 1024 * 1024
LOG2E = 1.4426950408889634
SSD_CHUNK = 128
CONV_HALO = 16
SSD_CONV_COLS = 512
DSA_QBLOCK = 128
KAUG_W = 128
FFN_ROWS = 512
FFN_FCHUNK = 256
PROJ_ROWS = 512
MERGE_ROWS = 512
MASK_NEG = -1e30


def _rms(x, g):
    ms = jnp.mean(x * x, axis=-1, keepdims=True)
    return x * lax.rsqrt(ms + EPS) * g


def _silu(x):
    return x / (1.0 + jnp.exp2(x * -LOG2E))


def _const_spec(shape):
    return pl.BlockSpec(shape, lambda *_: (0,) * len(shape), pipeline_mode=pl.Buffered(1))


def _params(*sem):
    return pltpu.CompilerParams(dimension_semantics=sem, vmem_limit_bytes=VMEM_LIMIT_BYTES)


def _ffn_kernel(x_ref, g_ref, wg_ref, wu_ref, wd_ref, *rest, final_norm):
    if final_norm:
        fg_ref, o_ref, acc_ref = rest
    else:
        o_ref, acc_ref = rest
    x = x_ref[...]
    xn = _rms(x, g_ref[...]).astype(BF16)
    d_ff = wg_ref.shape[1]
    for c in range(0, d_ff, FFN_FCHUNK):
        gate = jnp.dot(xn, wg_ref[:, c:c + FFN_FCHUNK], preferred_element_type=F32)
        up = jnp.dot(xn, wu_ref[:, c:c + FFN_FCHUNK], preferred_element_type=F32)
        act = (gate * jax.nn.sigmoid(gate) * up).astype(BF16)
        part = jnp.dot(act, wd_ref[c:c + FFN_FCHUNK, :], preferred_element_type=F32)
        if c == 0:
            acc_ref[...] = part
        else:
            acc_ref[...] += part
    y = x + FFN_RESIDUAL_SCALE * acc_ref[...]
    if final_norm:
        y = _rms(y, fg_ref[...])
    o_ref[...] = y


def _ffn(x, norm_g, wg, wu, wd, final_g=None):
    t, d = x.shape
    d_ff = wg.shape[1]
    rows = min(FFN_ROWS, t)
    in_specs = [pl.BlockSpec((rows, d), lambda i: (i, 0)), _const_spec((1, d)),
                _const_spec((d, d_ff)), _const_spec((d, d_ff)), _const_spec((d_ff, d))]
    args = [x, norm_g.reshape(1, d), wg, wu, wd]
    if final_g is not None:
        in_specs.append(_const_spec((1, d)))
        args.append(final_g.reshape(1, d))
    return pl.pallas_call(
        functools.partial(_ffn_kernel, final_norm=final_g is not None),
        out_shape=jax.ShapeDtypeStruct((t, d), F32),
        grid=(t // rows,),
        in_specs=in_specs,
        out_specs=pl.BlockSpec((rows, d), lambda i: (i, 0)),
        scratch_shapes=[pltpu.VMEM((rows, d), F32)],
        compiler_params=_params("parallel"),
        name="ffn_final" if final_g is not None else "ffn",
    )(*args)


def _dot_nt(a, b):
    return lax.dot_general(a, b, (((1,), (1,)), ((), ())), preferred_element_type=F32)


def _in_proj_kernel(x_ref, g_ref, wz, wxbc, wk, wgs, wga, wmisc, wqt, wqit, wvt, wwit,
                    oz, oxbc, ok, ogs, oga, odt, oki, oqt, oqit, ovt, owit, *, seq):
    xn = _rms(x_ref[...], g_ref[...]).astype(BF16)
    for w_ref, o_ref in ((wz, oz), (wxbc, oxbc), (wgs, ogs), (wga, oga)):
        n = w_ref.shape[1]
        step = min(n, 512)
        for c in range(0, n, step):
            o_ref[:, c:c + step] = jnp.dot(xn, w_ref[:, c:c + step], preferred_element_type=F32).astype(o_ref.dtype)
    rows = x_ref.shape[0]
    pos = (pl.program_id(0) * rows) % seq + lax.broadcasted_iota(jnp.int32, (rows, KAUG_W), 0)
    lane = lax.broadcasted_iota(jnp.int32, (rows, KAUG_W), 1) - ATT_HEAD_DIM
    chunk_f = lax.shift_right_logical(pos, DSA_QBLOCK.bit_length() - 1).astype(F32)
    row_f = (pos & (DSA_QBLOCK - 1)).astype(F32)
    pos_cols = jnp.where((lane >= 0) & (lane < 3), chunk_f,
                         jnp.where((lane >= 3) & (lane < 6), row_f,
                                   jnp.where((lane >= 6) & (lane < 9), 1.0, 0.0)))
    k_nat = jnp.dot(xn, wk[...], preferred_element_type=F32) * LOG2E
    zeros = jnp.zeros((rows, KAUG_W - ATT_HEAD_DIM), F32)
    for g in range(ATT_KV_HEADS):
        k_g = jnp.concatenate([k_nat[:, g * ATT_HEAD_DIM:(g + 1) * ATT_HEAD_DIM], zeros], axis=1)
        ok[:, g * KAUG_W:(g + 1) * KAUG_W] = (k_g + pos_cols).astype(BF16)
    misc = jnp.dot(xn, wmisc[...], preferred_element_type=F32)
    odt[...] = misc[:, :SSM_HEADS]
    oki[...] = misc[:, SSM_HEADS:SSM_HEADS + IDX_DIM].astype(BF16)
    for w_ref, o_ref in ((wqt, oqt), (wqit, oqit)):
        n = w_ref.shape[0]
        for c in range(0, n, 256):
            o_ref[c:c + 256, :] = _dot_nt(w_ref[c:c + 256, :], xn).astype(o_ref.dtype)
    owit[...] = _dot_nt(wwit[...], xn)[:IDX_HEADS, :]
    vt = _dot_nt(wvt[...], xn).astype(BF16)
    for j in range(ovt.shape[0]):
        ovt[j] = vt[:, j * DSA_QBLOCK:(j + 1) * DSA_QBLOCK]


def _in_proj(x1, norm_g, w_in, seq):
    t, d = x1.shape
    offs = [0]
    for s in IN_SIZES:
        offs.append(offs[-1] + s)
    seg = lambda i: w_in[:, offs[i]:offs[i + 1]]
    w_misc = jnp.concatenate([seg(2), seg(7), jnp.zeros((d, 128 - SSM_HEADS - IDX_DIM), w_in.dtype)], axis=1)
    natural = [seg(0), seg(1), seg(4), seg(9), seg(10), w_misc]
    w_idx_t = jnp.concatenate([seg(8).T, jnp.zeros((16 - IDX_HEADS, d), w_in.dtype)], axis=0)
    transposed = [seg(3).T, seg(6).T, seg(5).T, w_idx_t]
    weights = [w.astype(BF16) for w in natural + transposed]
    rows = min(PROJ_ROWS, t)
    nchunk = rows // DSA_QBLOCK
    row_spec = lambda n: pl.BlockSpec((rows, n), lambda i: (i, 0))
    col_spec = lambda n: pl.BlockSpec((n, rows), lambda i: (0, i))
    kv_w = ATT_KV_HEADS * ATT_HEAD_DIM
    widths = [IN_SIZES[0], IN_SIZES[1], ATT_KV_HEADS * KAUG_W, IN_SIZES[9], IN_SIZES[10]]
    out_shapes = [jax.ShapeDtypeStruct((t, n), BF16) for n in widths]
    out_shapes += [jax.ShapeDtypeStruct((t, SSM_HEADS), F32), jax.ShapeDtypeStruct((t, IDX_DIM), BF16),
                   jax.ShapeDtypeStruct((ATT_HEADS * ATT_HEAD_DIM, t), BF16),
                   jax.ShapeDtypeStruct((IDX_HEADS * IDX_DIM, t), BF16),
                   jax.ShapeDtypeStruct((t // DSA_QBLOCK, kv_w, DSA_QBLOCK), BF16),
                   jax.ShapeDtypeStruct((IDX_HEADS, t), F32)]
    out_specs = [row_spec(s.shape[1]) for s in out_shapes[:7]]
    out_specs += [col_spec(ATT_HEADS * ATT_HEAD_DIM), col_spec(IDX_HEADS * IDX_DIM),
                  pl.BlockSpec((nchunk, kv_w, DSA_QBLOCK), lambda i: (i, 0, 0)), col_spec(IDX_HEADS)]
    return pl.pallas_call(
        functools.partial(_in_proj_kernel, seq=seq),
        out_shape=out_shapes,
        grid=(t // rows,),
        in_specs=[row_spec(d), _const_spec((1, d))] + [_const_spec(w.shape) for w in weights],
        out_specs=out_specs,
        compiler_params=_params("parallel"),
        name="in_proj",
    )(x1, norm_g.reshape(1, d), *weights)


def _ssd_kernel(xbc_ref, z_ref, dt_ref, shift_ref, convw_ref, convb_ref, dtb_ref, ahead2_ref, dexp_ref, normw_ref,
                o_ref, win_ref, state_ref):
    q = SSD_CHUNK
    hd, n, g = SSM_HEAD_DIM, SSM_STATE, SSM_GROUPS
    heads = SSM_HEADS
    hpg = heads // g
    gw = hpg * hd
    c = pl.program_id(1)

    @pl.when(c == 0)
    def _():
        win_ref[0:CONV_HALO, :] = jnp.zeros((CONV_HALO, XBC_W), BF16)
        state_ref[...] = jnp.zeros_like(state_ref)

    x_cur = xbc_ref[...]
    win_ref[CONV_HALO:CONV_HALO + q, :] = x_cur
    pieces = []
    for c0 in range(0, XBC_W, SSD_CONV_COLS):
        cols = slice(c0, c0 + SSD_CONV_COLS)
        shifted = jnp.dot(shift_ref[...], win_ref[:, cols], preferred_element_type=F32)
        conv = convb_ref[:, cols] + convw_ref[CONV_K - 1:CONV_K, cols] * x_cur[:, cols].astype(F32)
        for k in range(CONV_K - 1):
            conv = conv + convw_ref[k:k + 1, cols] * shifted[k * q:(k + 1) * q, :]
        pieces.append(_silu(conv))
    win_ref[0:CONV_HALO, :] = x_cur[q - CONV_HALO:, :]
    xbc = jnp.concatenate(pieces, axis=1)
    xs = xbc[:, :D_INNER]
    bm = xbc[:, D_INNER:D_INNER + g * n]
    cm = xbc[:, D_INNER + g * n:]

    dt = jax.nn.softplus(dt_ref[...] + dtb_ref[...])
    a2 = dt * ahead2_ref[...]
    row = lax.broadcasted_iota(jnp.int32, (q, q), 0)
    col = lax.broadcasted_iota(jnp.int32, (q, q), 1)
    causal = row >= col
    tril = jnp.where(causal, 1.0, 0.0).astype(BF16)
    a2_cs = _split_dot_left(tril, a2)
    a2_cs_t = _transpose_heads(a2_cs)
    a2_last = a2_cs[q - 1:q, :]
    stack = jnp.concatenate([dt, jnp.exp2(a2_cs), dt * jnp.exp2(a2_last - a2_cs),
                             jnp.broadcast_to(jnp.exp2(a2_last), (16, heads))], axis=0)
    hi = stack.astype(BF16)
    lo = (stack - hi.astype(F32)).astype(BF16)
    lane = lax.broadcasted_iota(jnp.int32, (2 * heads, heads * hd), 1)
    src = lax.broadcasted_iota(jnp.int32, (2 * heads, heads * hd), 0)
    first = jnp.where(src >= heads, src - heads, src) * hd
    expand = jnp.where((lane >= first) & (lane < first + hd), 1.0, 0.0).astype(BF16)
    ex = jnp.dot(jnp.concatenate([hi, lo], axis=1), expand, preferred_element_type=F32)
    dt_x, in_decay_x, w_state_x, chunk_decay_x = ex[0:q], ex[q:2 * q], ex[2 * q:3 * q], ex[3 * q:3 * q + 1]

    prev = state_ref[...]
    prev_b = prev.astype(BF16)
    lane_pair = lax.broadcasted_iota(jnp.int32, (q, 2 * hd), 1)
    ys = []
    for gi in range(g):
        sl = slice(gi * gw, (gi + 1) * gw)
        bg = bm[:, gi * n:(gi + 1) * n]
        bg_b = bg.astype(BF16)
        cg_b = cm[:, gi * n:(gi + 1) * n].astype(BF16)
        cb = jnp.where(causal, _dot_nt(cg_b, bg_b), 0.0)
        xs_g = xs[:, sl]
        xdt_b = (xs_g * dt_x[:, sl]).astype(BF16)
        y_off = jnp.dot(cg_b, prev_b[:, sl], preferred_element_type=F32)
        pairs = []
        for pr in range(hpg // 2):
            ms = []
            for h in (gi * hpg + 2 * pr, gi * hpg + 2 * pr + 1):
                seg = jnp.broadcast_to(a2_cs[:, h:h + 1], (q, q)) - a2_cs_t[h:h + 1, :]
                ms.append((cb * jnp.exp2(jnp.minimum(seg, 0.0))).astype(BF16))
            xp = xdt_b[:, 2 * pr * hd:(2 * pr + 2) * hd]
            zero = jnp.zeros_like(xp)
            rhs = jnp.concatenate([jnp.where(lane_pair < hd, xp, zero), jnp.where(lane_pair >= hd, xp, zero)], axis=0)
            pairs.append(jnp.dot(jnp.concatenate(ms, axis=1), rhs, preferred_element_type=F32))
        ys.append(jnp.concatenate(pairs, axis=1) + y_off * in_decay_x[:, sl] + dexp_ref[:, sl] * xs_g)
        xw = (xs_g * w_state_x[:, sl]).astype(BF16)
        s_new = jnp.dot(bg.T.astype(BF16), xw, preferred_element_type=F32)
        state_ref[:, sl] = prev[:, sl] * chunk_decay_x[:, sl] + s_new
    y = jnp.concatenate(ys, axis=1)

    z = z_ref[...].astype(F32)
    yg = y * _silu(z)
    outs = []
    for gi in range(g):
        blk = yg[:, gi * gw:(gi + 1) * gw]
        ms = jnp.mean(blk * blk, axis=-1, keepdims=True)
        outs.append(blk * lax.rsqrt(ms + EPS))
    o_ref[...] = (jnp.concatenate(outs, axis=1) * normw_ref[...]).astype(o_ref.dtype)


def _transpose_heads(x):
    q, h = x.shape
    return jnp.concatenate([x, jnp.zeros((q, q - h), x.dtype)], axis=1).T[:h, :]


def _split_dot_left(b01, a):
    out = None
    rem = a
    for _ in range(3):
        hi = rem.astype(BF16)
        term = jnp.dot(b01, hi, preferred_element_type=F32)
        out = term if out is None else out + term
        rem = rem - hi.astype(F32)
    return out


def _conv_shift_matrix(q):
    s = np.zeros(((CONV_K - 1) * q, CONV_HALO + q), np.float32)
    for k in range(CONV_K - 1):
        s[k * q + np.arange(q), CONV_HALO + np.arange(q) - (CONV_K - 1) + k] = 1.0
    return jnp.asarray(s, BF16)


def _ssd(xbc, z, dt_raw, conv_w, conv_b, dt_bias, a_log, d_skip, norm_w, bsz, seq):
    q = SSD_CHUNK
    nc = seq // q
    a_head2 = (-jnp.exp(a_log.astype(F32)) * LOG2E).reshape(1, SSM_HEADS)
    d_exp = jnp.repeat(d_skip.astype(F32), SSM_HEAD_DIM).reshape(1, D_INNER)
    shift = _conv_shift_matrix(q)
    row_spec = lambda n: pl.BlockSpec((q, n), lambda b, c: (b * nc + c, 0))
    return pl.pallas_call(
        _ssd_kernel,
        out_shape=jax.ShapeDtypeStruct((bsz * seq, D_INNER), BF16),
        grid=(bsz, nc),
        in_specs=[row_spec(XBC_W), row_spec(D_INNER), row_spec(SSM_HEADS), _const_spec(shift.shape),
                  _const_spec((CONV_K, XBC_W)), _const_spec((1, XBC_W)), _const_spec((1, SSM_HEADS)),
                  _const_spec((1, SSM_HEADS)), _const_spec((1, D_INNER)), _const_spec((1, D_INNER))],
        out_specs=row_spec(D_INNER),
        scratch_shapes=[pltpu.VMEM((CONV_HALO + q, XBC_W), BF16), pltpu.VMEM((SSM_STATE, D_INNER), F32)],
        compiler_params=_params("parallel", "arbitrary"),
        name="ssd",
    )(xbc, z, dt_raw, shift, conv_w.astype(F32), conv_b.reshape(1, XBC_W).astype(F32),
      dt_bias.reshape(1, SSM_HEADS).astype(F32), a_head2, d_exp, norm_w.reshape(1, D_INNER).astype(F32))


def _sortable_to_f32(s):
    bits = jnp.where(s < 0, s ^ jnp.int32(-2 ** 31), ~s)
    return lax.bitcast_convert_type(bits, F32)


def _tile_count(mask):
    ones = jnp.where(mask, 1.0, 0.0)
    return jnp.sum(ones.reshape(ones.shape[0] // 8, 8, ones.shape[1]), axis=0)


def _np_bf16_pieces(x, n=3):
    out = []
    rem = np.float32(x)
    for _ in range(n):
        piece = np.float32(np.asarray(rem, dtype=jnp.bfloat16))
        out.append(piece)
        rem = np.float32(rem - piece)
    return out


def _bf16_pieces(x, n=3):
    out = []
    rem = x
    for _ in range(n):
        piece = rem.astype(BF16).astype(F32)
        out.append(piece)
        rem = rem - piece
    return out


def _dsa_kernel(qa_ref, qb_ref, qia_ref, qib_ref, wa_ref, wb_ref, k_ref, ki_ref, vt_ref, oa_ref, ob_ref,
                sc_ref, mask_ref, m_ref, acc_ref, qaug_ref, qis_ref, ws_ref, rel0_ref, rel1_ref, raw0_ref, raw1_ref,
                *, top_k, nq):
    tq = DSA_QBLOCK
    rep = ATT_HEADS // ATT_KV_HEADS
    hd = ATT_HEAD_DIM
    p = pl.program_id(1)
    n_a = p + 1
    nslot = nq + 1
    t_a = p * tq + lax.broadcasted_iota(jnp.int32, (1, tq), 1)
    t_b = (nq - 1 - p) * tq + lax.broadcasted_iota(jnp.int32, (1, tq), 1)
    key_in_tile = lax.broadcasted_iota(jnp.int32, (tq, tq), 0)
    kf = float(top_k)

    def slot_info(s):
        is_a = s <= p
        return jnp.where(is_a, 0, 1), jnp.minimum(jnp.where(is_a, s, s - n_a), nq - 1)

    for blk, (qi_ref, w_ref, q_ref, t_row) in enumerate(((qia_ref, wa_ref, qa_ref, t_a), (qib_ref, wb_ref, qb_ref, t_b))):
        qi = qi_ref[...]
        qis_ref[blk] = jnp.concatenate([qi[j * IDX_DIM:(j + 1) * IDX_DIM, :] for j in range(IDX_HEADS)], axis=1)
        ws_ref[blk] = w_ref[...] * (IDX_HEADS ** -0.5) * (IDX_DIM ** -0.5)
        row16 = lax.broadcasted_iota(jnp.int32, (16, tq), 0)
        for g in range(ATT_KV_HEADS):
            q_t = jnp.concatenate([q_ref[(g * rep + r) * hd:(g * rep + r + 1) * hd, :] for r in range(rep)], axis=1)
            bias_rows = []
            for r in range(rep):
                slope = np.float32(2.0 ** (-ALIBI_BASE * (g * rep + r + 1) / ATT_HEADS) * LOG2E)
                pieces = [float(v) for v in _np_bf16_pieces(slope)]
                terms = [tq * v for v in pieces] + pieces + _bf16_pieces(-(slope * t_row.astype(F32)))
                rows = jnp.zeros((16, tq), F32)
                for i, term in enumerate(terms):
                    rows = jnp.where(row16 == i, term, rows)
                bias_rows.append(rows)
            qaug_ref[blk * ATT_KV_HEADS + g] = jnp.concatenate(
                [q_t * (hd ** -0.5), jnp.concatenate(bias_rows, axis=1).astype(BF16),
                 jnp.zeros((KAUG_W - hd - 16, rep * tq), BF16)], axis=0)

    def issue_rel(s, rel_ref):
        blk, c = slot_info(s)
        kc = ki_ref[pl.ds(pl.multiple_of(c * tq, tq), tq), :]
        rel_ref[...] = jnp.dot(kc, qis_ref[blk], preferred_element_type=F32)

    def score_slot(s, rel_ref):
        blk, c = slot_info(s)
        w = ws_ref[blk]
        acc = w[0:1, :] * jnp.maximum(rel_ref[:, 0:tq], 0.0)
        for j in range(1, IDX_HEADS):
            acc = acc + w[j:j + 1, :] * jnp.maximum(rel_ref[:, j * tq:(j + 1) * tq], 0.0)
        sc_ref[s] = jnp.where(c * tq + key_in_tile <= jnp.where(blk == 0, t_a, t_b), acc, -jnp.inf)

    issue_rel(0, rel0_ref)

    def score_trip(j, carry):
        s0 = 2 * j
        s1 = jnp.minimum(s0 + 1, nslot - 1)
        issue_rel(s1, rel1_ref)
        score_slot(s0, rel0_ref)
        issue_rel(jnp.minimum(s0 + 2, nslot - 1), rel0_ref)
        score_slot(s1, rel1_ref)
        return carry

    lax.fori_loop(0, (nslot + 1) // 2, score_trip, 0, unroll=True)

    def both_counts(pred):
        acc_a = jnp.zeros((8, tq), F32)
        acc_b = jnp.zeros((8, tq), F32)
        for s in range(nslot):
            is_a = s <= p
            part = _tile_count(pred(s, sc_ref[s], is_a))
            if s == 0:
                acc_a = acc_a + part
            elif s >= nq // 2:
                acc_b = acc_b + part
            else:
                acc_a = acc_a + jnp.where(is_a, part, 0.0)
                acc_b = acc_b + jnp.where(is_a, 0.0, part)
        return jnp.sum(acc_a, axis=0, keepdims=True), jnp.sum(acc_b, axis=0, keepdims=True)

    def value_step(b, carry):
        pre_a, pre_b = carry
        bit = lax.shift_left(jnp.int32(1), 31 - b)
        cand_a, cand_b = pre_a | bit, pre_b | bit
        cf_a, cf_b = _sortable_to_f32(cand_a), _sortable_to_f32(cand_b)
        cnt_a, cnt_b = both_counts(lambda s, tile, is_a: tile >= jnp.where(is_a, cf_a, cf_b))
        return jnp.where(cnt_a >= kf, cand_a, pre_a), jnp.where(cnt_b >= kf, cand_b, pre_b)

    zero_row = jnp.zeros((1, tq), jnp.int32)
    pre_a, pre_b = lax.fori_loop(0, 32, value_step, (zero_row, zero_row))
    thr_a, thr_b = _sortable_to_f32(pre_a), _sortable_to_f32(pre_b)
    above_a, above_b = both_counts(lambda s, tile, is_a: tile > jnp.where(is_a, thr_a, thr_b))
    need_a, need_b = kf - above_a, kf - above_b

    def slot_key(s, is_a):
        return jnp.where(is_a, s, s - n_a) * tq + key_in_tile

    def index_step(b, carry):
        lo_a, lo_b = carry
        bit = lax.shift_left(jnp.int32(1), b)
        cand_a, cand_b = lo_a + bit, lo_b + bit
        cnt_a, cnt_b = both_counts(
            lambda s, tile, is_a: (tile == jnp.where(is_a, thr_a, thr_b))
            & (slot_key(s, is_a) < jnp.where(is_a, cand_a, cand_b)))
        return jnp.where(cnt_a < need_a, cand_a, lo_a), jnp.where(cnt_b < need_b, cand_b, lo_b)

    tie_a, tie_b = both_counts(lambda s, tile, is_a: tile == jnp.where(is_a, thr_a, thr_b))
    excess = jnp.maximum(jnp.max(tie_a - need_a), jnp.max(tie_b - need_b))
    nbits = (nq * tq - 1).bit_length()
    all_keys = jnp.full((1, tq), nq * tq, jnp.int32)
    last_a, last_b = lax.cond(
        excess > 0.0,
        lambda: lax.fori_loop(0, nbits, lambda b, c: index_step(nbits - 1 - b, c), (zero_row, zero_row)),
        lambda: (all_keys, all_keys))

    def finish(t_row, thr, last):
        early = t_row < top_k
        return jnp.where(early, -jnp.finfo(F32).max, thr), jnp.where(early, nq * tq, last)

    thr_a, last_a = finish(t_a, thr_a, last_a)
    thr_b, last_b = finish(t_b, thr_b, last_b)

    n_b = nq - p
    a_slots = n_a + (n_a & 1)

    def att_slot_info(s):
        is_a = s < a_slots
        local = jnp.where(is_a, s, s - a_slots)
        count = jnp.where(is_a, n_a, n_b)
        chunk = jnp.minimum(local, count - 1)
        return jnp.where(is_a, 0, 1), chunk, jnp.where(is_a, chunk, n_a + chunk), local >= count

    for s in range(nslot + 1):
        blk, chunk, src, spare = att_slot_info(s)
        is_a = blk == 0
        thr = jnp.where(is_a, thr_a, thr_b)
        tile = sc_ref[src]
        key = chunk * tq + key_in_tile
        sel = (tile > thr) | ((tile == thr) & (key <= jnp.where(is_a, last_a, last_b)))
        mask_ref[s] = jnp.where(sel & jnp.logical_not(spare), 0.0, MASK_NEG)

    ones_rows = jnp.where(lax.broadcasted_iota(jnp.int32, (16, 2 * tq), 0) == 0, 1.0, 0.0).astype(BF16)
    m_ref[...] = jnp.full(m_ref.shape, -jnp.inf, F32)
    acc_ref[...] = jnp.zeros(acc_ref.shape, F32)

    def issue_qk(pair, g, raw_ref):
        for u in range(2):
            blk, c, _, _ = att_slot_info(2 * pair + u)
            kc = k_ref[pl.ds(pl.multiple_of(c * tq, tq), tq), g * KAUG_W:(g + 1) * KAUG_W]
            raw_ref[u, g] = jnp.dot(kc, qaug_ref[blk * ATT_KV_HEADS + g], preferred_element_type=F32)

    def attend_group(pair, g, raw_ref):
        blk, c0, _, _ = att_slot_info(2 * pair)
        _, c1, _, _ = att_slot_info(2 * pair + 1)
        slot = blk * ATT_KV_HEADS + g
        m_old = m_ref[slot]
        mask = mask_ref[2 * pair]
        logits = jnp.concatenate([raw_ref[0, g, :, r * tq:(r + 1) * tq] + mask for r in range(rep)], axis=1)
        m_mid = jnp.maximum(m_old, jnp.max(logits, axis=0, keepdims=True))
        prob0 = jnp.exp2(logits - m_mid).astype(BF16)
        mask = mask_ref[2 * pair + 1]
        logits = jnp.concatenate([raw_ref[1, g, :, r * tq:(r + 1) * tq] + mask for r in range(rep)], axis=1)
        m_new = jnp.maximum(m_mid, jnp.max(logits, axis=0, keepdims=True))
        prob1 = jnp.exp2(logits - m_new).astype(BF16)
        prob = jnp.concatenate([prob0 * jnp.exp2(m_mid - m_new).astype(BF16), prob1], axis=0)
        v_pair = jnp.concatenate([vt_ref[c0, g * hd:(g + 1) * hd, :], vt_ref[c1, g * hd:(g + 1) * hd, :]], axis=1)
        v_aug = jnp.concatenate([v_pair, ones_rows], axis=0)
        pv = jnp.dot(v_aug, prob, preferred_element_type=F32)
        acc_ref[slot] = acc_ref[slot] * jnp.exp2(m_old - m_new) + pv
        m_ref[slot] = m_new

    npair = (nslot + 1) // 2
    for g in range(ATT_KV_HEADS):
        issue_qk(0, g, raw0_ref)
    for j in range(npair):
        cur, nxt = (raw0_ref, raw1_ref) if j % 2 == 0 else (raw1_ref, raw0_ref)
        for g in range(ATT_KV_HEADS):
            if j + 1 < npair:
                issue_qk(j + 1, g, nxt)
            attend_group(j, g, cur)

    for blk, o_ref in enumerate((oa_ref, ob_ref)):
        outs = []
        for g in range(ATT_KV_HEADS):
            acc = acc_ref[blk * ATT_KV_HEADS + g]
            o_t = acc[:hd, :] / acc[hd:hd + 1, :]
            outs.extend(o_t[:, r * tq:(r + 1) * tq] for r in range(rep))
        o_ref[...] = jnp.concatenate(outs, axis=0).T.astype(o_ref.dtype)


def _dsa(q_t, k, v_t, qi_t, k_idx, wi_t, bsz, seq):
    tq = DSA_QBLOCK
    nq = seq // tq
    half = nq // 2
    top_k = min(TOPK_MAX, seq // 4)
    width = ATT_HEADS * ATT_HEAD_DIM
    gq = (ATT_HEADS // ATT_KV_HEADS) * tq
    col_a = lambda n: pl.BlockSpec((n, tq), lambda b, p: (0, b * nq + p))
    col_b = lambda n: pl.BlockSpec((n, tq), lambda b, p: (0, b * nq + nq - 1 - p))
    full = lambda n: pl.BlockSpec((seq, n), lambda b, p: (b, 0))
    out_a, out_b = pl.pallas_call(
        functools.partial(_dsa_kernel, top_k=top_k, nq=nq),
        out_shape=[jax.ShapeDtypeStruct((bsz, half * tq, width), BF16)] * 2,
        grid=(bsz, half),
        in_specs=[col_a(q_t.shape[0]), col_b(q_t.shape[0]), col_a(qi_t.shape[0]), col_b(qi_t.shape[0]),
                  col_a(wi_t.shape[0]), col_b(wi_t.shape[0]), full(k.shape[1]), full(k_idx.shape[1]),
                  pl.BlockSpec((nq, v_t.shape[1], tq), lambda b, p: (b, 0, 0))],
        out_specs=[pl.BlockSpec((None, tq, width), lambda b, p: (b, p, 0)),
                   pl.BlockSpec((None, tq, width), lambda b, p: (b, half - 1 - p, 0))],
        scratch_shapes=[pltpu.VMEM((nq + 1, tq, tq), F32),
                        pltpu.VMEM((nq + 2, tq, tq), F32),
                        pltpu.VMEM((2 * ATT_KV_HEADS, 1, gq), F32),
                        pltpu.VMEM((2 * ATT_KV_HEADS, ATT_HEAD_DIM + 16, gq), F32),
                        pltpu.VMEM((2 * ATT_KV_HEADS, KAUG_W, gq), BF16),
                        pltpu.VMEM((2, IDX_DIM, IDX_HEADS * tq), BF16),
                        pltpu.VMEM((2, IDX_HEADS, tq), F32),
                        pltpu.VMEM((tq, IDX_HEADS * tq), F32), pltpu.VMEM((tq, IDX_HEADS * tq), F32),
                        pltpu.VMEM((2, ATT_KV_HEADS, tq, gq), F32),
                        pltpu.VMEM((2, ATT_KV_HEADS, tq, gq), F32)],
        compiler_params=_params("parallel", "arbitrary"),
        name="dsa",
    )(q_t, q_t, qi_t, qi_t, wi_t, wi_t, k, k_idx, v_t)
    return jnp.concatenate([out_a, out_b], axis=1).reshape(bsz * seq, width)


def _merge_kernel(x_ref, ya_ref, yb_ref, ga_ref, gb_ref, wa_ref, wb_ref, wo_ref, o_ref):
    pa = jnp.dot(ya_ref[...], wa_ref[...], preferred_element_type=F32)
    pb = jnp.dot(yb_ref[...], wb_ref[...], preferred_element_type=F32)
    merged = (jax.nn.sigmoid(ga_ref[...].astype(F32)) * pa + jax.nn.sigmoid(gb_ref[...].astype(F32)) * pb)
    o_ref[...] = x_ref[...] + jnp.dot(merged.astype(BF16), wo_ref[...], preferred_element_type=F32)


def _merge(x1, y_ssm, y_att, g_ssm, g_att, w_a, w_b, w_o):
    t, d = x1.shape
    rows = min(MERGE_ROWS, t)
    row_spec = lambda n: pl.BlockSpec((rows, n), lambda i: (i, 0))
    return pl.pallas_call(
        _merge_kernel,
        out_shape=jax.ShapeDtypeStruct((t, d), F32),
        grid=(t // rows,),
        in_specs=[row_spec(d), row_spec(y_ssm.shape[1]), row_spec(y_att.shape[1]), row_spec(d), row_spec(d),
                  _const_spec(w_a.shape), _const_spec(w_b.shape), _const_spec(w_o.shape)],
        out_specs=row_spec(d),
        compiler_params=_params("parallel"),
        name="merge",
    )(x1, y_ssm, y_att, g_ssm, g_att, w_a, w_b, w_o)


def kernel(x, ffn1_norm, ffn1_w_gate, ffn1_w_up, ffn1_w_down, mix_norm, w_in, conv_w, conv_b, dt_bias, a_log,
           d_skip, ssm_norm, w_branch_ssm, w_branch_attn, w_out, ffn2_norm, ffn2_w_gate, ffn2_w_up, ffn2_w_down,
           final_norm):
    bsz, seq, d = x.shape
    depth = ffn1_norm.shape[0]
    h = x.reshape(bsz * seq, d)
    for l in range(depth):
        last = l == depth - 1
        h = _ffn(h, ffn1_norm[l], ffn1_w_gate[l].astype(BF16), ffn1_w_up[l].astype(BF16),
                 ffn1_w_down[l].astype(BF16))
        z, xbc, k, g_ssm, g_att, dt_raw, k_idx, q_t, qi_t, v_t, wi_t = _in_proj(h, mix_norm[l], w_in[l], seq)
        y_ssm = _ssd(xbc, z, dt_raw, conv_w[l], conv_b[l], dt_bias[l], a_log[l], d_skip[l], ssm_norm[l], bsz, seq)
        y_att = _dsa(q_t, k, v_t, qi_t, k_idx, wi_t, bsz, seq)
        h = _merge(h, y_ssm, y_att, g_ssm, g_att, w_branch_ssm[l].astype(BF16), w_branch_attn[l].astype(BF16),
                   w_out[l].astype(BF16))
        h = _ffn(h, ffn2_norm[l], ffn2_w_gate[l].astype(BF16), ffn2_w_up[l].astype(BF16),
                 ffn2_w_down[l].astype(BF16), final_g=final_norm if last else None)
    return h.reshape(bsz, seq, d)
```

```python
import functools

import jax
import jax.numpy as jnp
import numpy as np
from jax import lax
from jax.experimental import pallas as pl
from jax.experimental.pallas import tpu as pltpu

F32 = jnp.float32
BF16 = jnp.bfloat16

D_MODEL = 1024
D_INNER = 2048
SSM_HEAD_DIM = 64
SSM_HEADS = 32
SSM_GROUPS = 4
SSM_STATE = 128
CONV_K = 4
XBC_W = D_INNER + 2 * SSM_GROUPS * SSM_STATE
ATT_HEADS = 16
ATT_HEAD_DIM = 64
ATT_KV_HEADS = 4
IDX_HEADS = 8
IDX_DIM = 64
TOPK_MAX = 256
ALIBI_BASE = 8.0
D_FF = 2816
FFN_RESIDUAL_SCALE = 0.5
EPS = 1e-6

IN_SIZES = (D_INNER, XBC_W, SSM_HEADS, ATT_HEADS * ATT_HEAD_DIM, ATT_KV_HEADS * ATT_HEAD_DIM,
            ATT_KV_HEADS * ATT_HEAD_DIM, IDX_HEADS * IDX_DIM, IDX_DIM, IDX_HEADS, D_MODEL, D_MODEL)

VMEM_LIMIT_BYTES = 56 * 1024 * 1024
LOG2E = 1.4426950408889634
SSD_CHUNK = 128
CONV_HALO = 16
SSD_CONV_COLS = 512
DSA_QBLOCK = 128
KAUG_W = 128
FFN_ROWS = 512
FFN_FCHUNK = 256
PROJ_ROWS = 512
MASK_NEG = -1e30


def _rms(x, g):
    ms = jnp.mean(x * x, axis=-1, keepdims=True)
    return x * lax.rsqrt(ms + EPS) * g


def _silu(x):
    return x / (1.0 + jnp.exp2(x * -LOG2E))


def _const_spec(shape):
    return pl.BlockSpec(shape, lambda *_: (0,) * len(shape), pipeline_mode=pl.Buffered(1))


def _params(*sem):
    return pltpu.CompilerParams(dimension_semantics=sem, vmem_limit_bytes=VMEM_LIMIT_BYTES)


def _ffn_kernel(x_ref, *rest, merge, final_norm):
    rest = list(rest)
    x = x_ref[...]
    if merge:
        ya_ref, yb_ref, ga_ref, gb_ref, wa_ref, wb_ref, wo_ref = rest[:7]
        rest = rest[7:]
        pa = jnp.dot(ya_ref[...], wa_ref[...], preferred_element_type=F32)
        pb = jnp.dot(yb_ref[...], wb_ref[...], preferred_element_type=F32)
        merged = jax.nn.sigmoid(ga_ref[...].astype(F32)) * pa + jax.nn.sigmoid(gb_ref[...].astype(F32)) * pb
        x = x + jnp.dot(merged.astype(BF16), wo_ref[...], preferred_element_type=F32)
    g_ref, wg_ref, wu_ref, wd_ref = rest[:4]
    rest = rest[4:]
    if final_norm:
        fg_ref = rest.pop(0)
    o_ref, acc_ref = rest
    xn = _rms(x, g_ref[...]).astype(BF16)
    d_ff = wg_ref.shape[1]
    for c in range(0, d_ff, FFN_FCHUNK):
        gate = jnp.dot(xn, wg_ref[:, c:c + FFN_FCHUNK], preferred_element_type=F32)
        up = jnp.dot(xn, wu_ref[:, c:c + FFN_FCHUNK], preferred_element_type=F32)
        act = (gate * jax.nn.sigmoid(gate) * up).astype(BF16)
        part = jnp.dot(act, wd_ref[c:c + FFN_FCHUNK, :], preferred_element_type=F32)
        if c == 0:
            acc_ref[...] = part
        else:
            acc_ref[...] += part
    y = x + FFN_RESIDUAL_SCALE * acc_ref[...]
    if final_norm:
        y = _rms(y, fg_ref[...])
    o_ref[...] = y


def _ffn(x, norm_g, wg, wu, wd, final_g=None, merge=None):
    t, d = x.shape
    d_ff = wg.shape[1]
    rows = min(FFN_ROWS, t)
    row_spec = lambda n: pl.BlockSpec((rows, n), lambda i: (i, 0))
    in_specs = [row_spec(d)]
    args = [x]
    if merge is not None:
        in_specs += [row_spec(a.shape[1]) for a in merge[:4]] + [_const_spec(w.shape) for w in merge[4:]]
        args += list(merge)
    in_specs += [_const_spec((1, d)), _const_spec((d, d_ff)), _const_spec((d, d_ff)), _const_spec((d_ff, d))]
    args += [norm_g.reshape(1, d), wg, wu, wd]
    if final_g is not None:
        in_specs.append(_const_spec((1, d)))
        args.append(final_g.reshape(1, d))
    return pl.pallas_call(
        functools.partial(_ffn_kernel, merge=merge is not None, final_norm=final_g is not None),
        out_shape=jax.ShapeDtypeStruct((t, d), F32),
        grid=(t // rows,),
        in_specs=in_specs,
        out_specs=pl.BlockSpec((rows, d), lambda i: (i, 0)),
        scratch_shapes=[pltpu.VMEM((rows, d), F32)],
        compiler_params=_params("parallel"),
        name="merge_ffn" if merge is not None else "ffn",
    )(*args)


def _dot_nt(a, b):
    return lax.dot_general(a, b, (((1,), (1,)), ((), ())), preferred_element_type=F32)


def _in_proj_kernel(x_ref, g_ref, wz, wxbc, wk, wgs, wga, wmisc, wqt, wqit, wvt, wwit,
                    oz, oxbc, ok, ogs, oga, odt, oki, oqt, oqit, ovt, owit, *, seq):
    xn = _rms(x_ref[...], g_ref[...]).astype(BF16)
    for w_ref, o_ref in ((wz, oz), (wxbc, oxbc), (wgs, ogs), (wga, oga)):
        n = w_ref.shape[1]
        step = min(n, 512)
        for c in range(0, n, step):
            o_ref[:, c:c + step] = jnp.dot(xn, w_ref[:, c:c + step], preferred_element_type=F32).astype(o_ref.dtype)
    rows = x_ref.shape[0]
    pos = (pl.program_id(0) * rows) % seq + lax.broadcasted_iota(jnp.int32, (rows, KAUG_W), 0)
    lane = lax.broadcasted_iota(jnp.int32, (rows, KAUG_W), 1) - ATT_HEAD_DIM
    chunk_f = lax.shift_right_logical(pos, DSA_QBLOCK.bit_length() - 1).astype(F32)
    row_f = (pos & (DSA_QBLOCK - 1)).astype(F32)
    pos_cols = jnp.where((lane >= 0) & (lane < 3), chunk_f,
                         jnp.where((lane >= 3) & (lane < 6), row_f,
                                   jnp.where((lane >= 6) & (lane < 9), 1.0, 0.0)))
    k_nat = jnp.dot(xn, wk[...], preferred_element_type=F32) * LOG2E
    zeros = jnp.zeros((rows, KAUG_W - ATT_HEAD_DIM), F32)
    for g in range(ATT_KV_HEADS):
        k_g = jnp.concatenate([k_nat[:, g * ATT_HEAD_DIM:(g + 1) * ATT_HEAD_DIM], zeros], axis=1)
        ok[:, g * KAUG_W:(g + 1) * KAUG_W] = (k_g + pos_cols).astype(BF16)
    misc = jnp.dot(xn, wmisc[...], preferred_element_type=F32)
    odt[...] = misc[:, :SSM_HEADS]
    oki[...] = misc[:, SSM_HEADS:SSM_HEADS + IDX_DIM].astype(BF16)
    for w_ref, o_ref in ((wqt, oqt), (wqit, oqit)):
        n = w_ref.shape[0]
        for c in range(0, n, 256):
            o_ref[c:c + 256, :] = _dot_nt(w_ref[c:c + 256, :], xn).astype(o_ref.dtype)
    owit[...] = _dot_nt(wwit[...], xn)[:IDX_HEADS, :]
    vt = _dot_nt(wvt[...], xn).astype(BF16)
    for j in range(ovt.shape[0]):
        ovt[j] = vt[:, j * DSA_QBLOCK:(j + 1) * DSA_QBLOCK]


def _in_proj(x1, norm_g, w_in, seq):
    t, d = x1.shape
    offs = [0]
    for s in IN_SIZES:
        offs.append(offs[-1] + s)
    seg = lambda i: w_in[:, offs[i]:offs[i + 1]]
    w_misc = jnp.concatenate([seg(2), seg(7), jnp.zeros((d, 128 - SSM_HEADS - IDX_DIM), w_in.dtype)], axis=1)
    natural = [seg(0), seg(1), seg(4), seg(9), seg(10), w_misc]
    w_idx_t = jnp.concatenate([seg(8).T, jnp.zeros((16 - IDX_HEADS, d), w_in.dtype)], axis=0)
    transposed = [seg(3).T, seg(6).T, seg(5).T, w_idx_t]
    weights = [w.astype(BF16) for w in natural + transposed]
    rows = min(PROJ_ROWS, t)
    nchunk = rows // DSA_QBLOCK
    row_spec = lambda n: pl.BlockSpec((rows, n), lambda i: (i, 0))
    col_spec = lambda n: pl.BlockSpec((n, rows), lambda i: (0, i))
    kv_w = ATT_KV_HEADS * ATT_HEAD_DIM
    widths = [IN_SIZES[0], IN_SIZES[1], ATT_KV_HEADS * KAUG_W, IN_SIZES[9], IN_SIZES[10]]
    out_shapes = [jax.ShapeDtypeStruct((t, n), BF16) for n in widths]
    out_shapes += [jax.ShapeDtypeStruct((t, SSM_HEADS), F32), jax.ShapeDtypeStruct((t, IDX_DIM), BF16),
                   jax.ShapeDtypeStruct((ATT_HEADS * ATT_HEAD_DIM, t), BF16),
                   jax.ShapeDtypeStruct((IDX_HEADS * IDX_DIM, t), BF16),
                   jax.ShapeDtypeStruct((t // DSA_QBLOCK, kv_w, DSA_QBLOCK), BF16),
                   jax.ShapeDtypeStruct((IDX_HEADS, t), F32)]
    out_specs = [row_spec(s.shape[1]) for s in out_shapes[:7]]
    out_specs += [col_spec(ATT_HEADS * ATT_HEAD_DIM), col_spec(IDX_HEADS * IDX_DIM),
                  pl.BlockSpec((nchunk, kv_w, DSA_QBLOCK), lambda i: (i, 0, 0)), col_spec(IDX_HEADS)]
    return pl.pallas_call(
        functools.partial(_in_proj_kernel, seq=seq),
        out_shape=out_shapes,
        grid=(t // rows,),
        in_specs=[row_spec(d), _const_spec((1, d))] + [_const_spec(w.shape) for w in weights],
        out_specs=out_specs,
        compiler_params=_params("parallel"),
        name="in_proj",
    )(x1, norm_g.reshape(1, d), *weights)


def _ssd_kernel(xbc_ref, z_ref, dt_ref, shift_ref, convw_ref, convb_ref, dtb_ref, ahead2_ref, dexp_ref, normw_ref,
                o_ref, win_ref, state_ref):
    q = SSD_CHUNK
    hd, n, g = SSM_HEAD_DIM, SSM_STATE, SSM_GROUPS
    heads = SSM_HEADS
    hpg = heads // g
    gw = hpg * hd
    c = pl.program_id(1)

    @pl.when(c == 0)
    def _():
        win_ref[0:CONV_HALO, :] = jnp.zeros((CONV_HALO, XBC_W), BF16)
        state_ref[...] = jnp.zeros_like(state_ref)

    x_cur = xbc_ref[...]
    win_ref[CONV_HALO:CONV_HALO + q, :] = x_cur
    pieces = []
    for c0 in range(0, XBC_W, SSD_CONV_COLS):
        cols = slice(c0, c0 + SSD_CONV_COLS)
        shifted = jnp.dot(shift_ref[...], win_ref[:, cols], preferred_element_type=F32)
        conv = convb_ref[:, cols] + convw_ref[CONV_K - 1:CONV_K, cols] * x_cur[:, cols].astype(F32)
        for k in range(CONV_K - 1):
            conv = conv + convw_ref[k:k + 1, cols] * shifted[k * q:(k + 1) * q, :]
        pieces.append(_silu(conv))
    win_ref[0:CONV_HALO, :] = x_cur[q - CONV_HALO:, :]
    xbc = jnp.concatenate(pieces, axis=1)
    xs = xbc[:, :D_INNER]
    bm = xbc[:, D_INNER:D_INNER + g * n]
    cm = xbc[:, D_INNER + g * n:]

    dt = jax.nn.softplus(dt_ref[...] + dtb_ref[...])
    a2 = dt * ahead2_ref[...]
    row = lax.broadcasted_iota(jnp.int32, (q, q), 0)
    col = lax.broadcasted_iota(jnp.int32, (q, q), 1)
    causal = row >= col
    tril = jnp.where(causal, 1.0, 0.0).astype(BF16)
    a2_cs = _split_dot_left(tril, a2)
    a2_cs_t = _transpose_heads(a2_cs)
    a2_last = a2_cs[q - 1:q, :]
    stack = jnp.concatenate([dt, jnp.exp2(a2_cs), dt * jnp.exp2(a2_last - a2_cs),
                             jnp.broadcast_to(jnp.exp2(a2_last), (16, heads))], axis=0)
    hi = stack.astype(BF16)
    lo = (stack - hi.astype(F32)).astype(BF16)
    lane = lax.broadcasted_iota(jnp.int32, (2 * heads, heads * hd), 1)
    src = lax.broadcasted_iota(jnp.int32, (2 * heads, heads * hd), 0)
    first = jnp.where(src >= heads, src - heads, src) * hd
    expand = jnp.where((lane >= first) & (lane < first + hd), 1.0, 0.0).astype(BF16)
    ex = jnp.dot(jnp.concatenate([hi, lo], axis=1), expand, preferred_element_type=F32)
    dt_x, in_decay_x, w_state_x, chunk_decay_x = ex[0:q], ex[q:2 * q], ex[2 * q:3 * q], ex[3 * q:3 * q + 1]

    prev = state_ref[...]
    prev_b = prev.astype(BF16)
    lane_pair = lax.broadcasted_iota(jnp.int32, (q, 2 * hd), 1)
    ys = []
    for gi in range(g):
        sl = slice(gi * gw, (gi + 1) * gw)
        bg = bm[:, gi * n:(gi + 1) * n]
        bg_b = bg.astype(BF16)
        cg_b = cm[:, gi * n:(gi + 1) * n].astype(BF16)
        cb = jnp.where(causal, _dot_nt(cg_b, bg_b), 0.0)
        xs_g = xs[:, sl]
        xdt_b = (xs_g * dt_x[:, sl]).astype(BF16)
        y_off = jnp.dot(cg_b, prev_b[:, sl], preferred_element_type=F32)
        pairs = []
        for pr in range(hpg // 2):
            ms = []
            for h in (gi * hpg + 2 * pr, gi * hpg + 2 * pr + 1):
                seg = jnp.broadcast_to(a2_cs[:, h:h + 1], (q, q)) - a2_cs_t[h:h + 1, :]
                ms.append((cb * jnp.exp2(jnp.minimum(seg, 0.0))).astype(BF16))
            xp = xdt_b[:, 2 * pr * hd:(2 * pr + 2) * hd]
            zero = jnp.zeros_like(xp)
            rhs = jnp.concatenate([jnp.where(lane_pair < hd, xp, zero), jnp.where(lane_pair >= hd, xp, zero)], axis=0)
            pairs.append(jnp.dot(jnp.concatenate(ms, axis=1), rhs, preferred_element_type=F32))
        ys.append(jnp.concatenate(pairs, axis=1) + y_off * in_decay_x[:, sl] + dexp_ref[:, sl] * xs_g)
        xw = (xs_g * w_state_x[:, sl]).astype(BF16)
        s_new = jnp.dot(bg.T.astype(BF16), xw, preferred_element_type=F32)
        state_ref[:, sl] = prev[:, sl] * chunk_decay_x[:, sl] + s_new
    y = jnp.concatenate(ys, axis=1)

    z = z_ref[...].astype(F32)
    yg = y * _silu(z)
    outs = []
    for gi in range(g):
        blk = yg[:, gi * gw:(gi + 1) * gw]
        ms = jnp.mean(blk * blk, axis=-1, keepdims=True)
        outs.append(blk * lax.rsqrt(ms + EPS))
    o_ref[...] = (jnp.concatenate(outs, axis=1) * normw_ref[...]).astype(o_ref.dtype)


def _transpose_heads(x):
    q, h = x.shape
    return jnp.concatenate([x, jnp.zeros((q, q - h), x.dtype)], axis=1).T[:h, :]


def _split_dot_left(b01, a):
    out = None
    rem = a
    for _ in range(3):
        hi = rem.astype(BF16)
        term = jnp.dot(b01, hi, preferred_element_type=F32)
        out = term if out is None else out + term
        rem = rem - hi.astype(F32)
    return out


def _conv_shift_matrix(q):
    s = np.zeros(((CONV_K - 1) * q, CONV_HALO + q), np.float32)
    for k in range(CONV_K - 1):
        s[k * q + np.arange(q), CONV_HALO + np.arange(q) - (CONV_K - 1) + k] = 1.0
    return jnp.asarray(s, BF16)


def _ssd(xbc, z, dt_raw, conv_w, conv_b, dt_bias, a_log, d_skip, norm_w, bsz, seq):
    q = SSD_CHUNK
    nc = seq // q
    a_head2 = (-jnp.exp(a_log.astype(F32)) * LOG2E).reshape(1, SSM_HEADS)
    d_exp = jnp.repeat(d_skip.astype(F32), SSM_HEAD_DIM).reshape(1, D_INNER)
    shift = _conv_shift_matrix(q)
    row_spec = lambda n: pl.BlockSpec((q, n), lambda b, c: (b * nc + c, 0))
    return pl.pallas_call(
        _ssd_kernel,
        out_shape=jax.ShapeDtypeStruct((bsz * seq, D_INNER), BF16),
        grid=(bsz, nc),
        in_specs=[row_spec(XBC_W), row_spec(D_INNER), row_spec(SSM_HEADS), _const_spec(shift.shape),
                  _const_spec((CONV_K, XBC_W)), _const_spec((1, XBC_W)), _const_spec((1, SSM_HEADS)),
                  _const_spec((1, SSM_HEADS)), _const_spec((1, D_INNER)), _const_spec((1, D_INNER))],
        out_specs=row_spec(D_INNER),
        scratch_shapes=[pltpu.VMEM((CONV_HALO + q, XBC_W), BF16), pltpu.VMEM((SSM_STATE, D_INNER), F32)],
        compiler_params=_params("parallel", "arbitrary"),
        name="ssd",
    )(xbc, z, dt_raw, shift, conv_w.astype(F32), conv_b.reshape(1, XBC_W).astype(F32),
      dt_bias.reshape(1, SSM_HEADS).astype(F32), a_head2, d_exp, norm_w.reshape(1, D_INNER).astype(F32))


def _sortable_to_f32(s):
    bits = jnp.where(s < 0, s ^ jnp.int32(-2 ** 31), ~s)
    return lax.bitcast_convert_type(bits, F32)


def _tile_count(mask):
    ones = jnp.where(mask, 1.0, 0.0)
    return jnp.sum(ones.reshape(ones.shape[0] // 8, 8, ones.shape[1]), axis=0)


def _np_bf16_pieces(x, n=3):
    out = []
    rem = np.float32(x)
    for _ in range(n):
        piece = np.float32(np.asarray(rem, dtype=jnp.bfloat16))
        out.append(piece)
        rem = np.float32(rem - piece)
    return out


def _bf16_pieces(x, n=3):
    out = []
    rem = x
    for _ in range(n):
        piece = rem.astype(BF16).astype(F32)
        out.append(piece)
        rem = rem - piece
    return out


def _dsa_kernel(qa_ref, qb_ref, qia_ref, qib_ref, wa_ref, wb_ref, k_ref, ki_ref, vt_ref, oa_ref, ob_ref,
                sc_ref, mask_ref, m_ref, acc_ref, qaug_ref, qis_ref, ws_ref, rel0_ref, rel1_ref, raw0_ref, raw1_ref,
                *, top_k, nq):
    tq = DSA_QBLOCK
    rep = ATT_HEADS // ATT_KV_HEADS
    hd = ATT_HEAD_DIM
    p = pl.program_id(1)
    n_a = p + 1
    nslot = nq + 1
    t_a = p * tq + lax.broadcasted_iota(jnp.int32, (1, tq), 1)
    t_b = (nq - 1 - p) * tq + lax.broadcasted_iota(jnp.int32, (1, tq), 1)
    key_in_tile = lax.broadcasted_iota(jnp.int32, (tq, tq), 0)
    kf = float(top_k)

    def slot_info(s):
        is_a = s <= p
        return jnp.where(is_a, 0, 1), jnp.minimum(jnp.where(is_a, s, s - n_a), nq - 1)

    for blk, (qi_ref, w_ref, q_ref, t_row) in enumerate(((qia_ref, wa_ref, qa_ref, t_a), (qib_ref, wb_ref, qb_ref, t_b))):
        qi = qi_ref[...]
        qis_ref[blk] = jnp.concatenate([qi[j * IDX_DIM:(j + 1) * IDX_DIM, :] for j in range(IDX_HEADS)], axis=1)
        ws_ref[blk] = w_ref[...] * (IDX_HEADS ** -0.5) * (IDX_DIM ** -0.5)
        row16 = lax.broadcasted_iota(jnp.int32, (16, tq), 0)
        for g in range(ATT_KV_HEADS):
            q_t = jnp.concatenate([q_ref[(g * rep + r) * hd:(g * rep + r + 1) * hd, :] for r in range(rep)], axis=1)
            bias_rows = []
            for r in range(rep):
                slope = np.float32(2.0 ** (-ALIBI_BASE * (g * rep + r + 1) / ATT_HEADS) * LOG2E)
                pieces = [float(v) for v in _np_bf16_pieces(slope)]
                terms = [tq * v for v in pieces] + pieces + _bf16_pieces(-(slope * t_row.astype(F32)))
                rows = jnp.zeros((16, tq), F32)
                for i, term in enumerate(terms):
                    rows = jnp.where(row16 == i, term, rows)
                bias_rows.append(rows)
            qaug_ref[blk * ATT_KV_HEADS + g] = jnp.concatenate(
                [q_t * (hd ** -0.5), jnp.concatenate(bias_rows, axis=1).astype(BF16),
                 jnp.zeros((KAUG_W - hd - 16, rep * tq), BF16)], axis=0)

    def issue_rel(s, rel_ref):
        blk, c = slot_info(s)
        kc = ki_ref[pl.ds(pl.multiple_of(c * tq, tq), tq), :]
        rel_ref[...] = jnp.dot(kc, qis_ref[blk], preferred_element_type=F32)

    def score_slot(s, rel_ref):
        blk, c = slot_info(s)
        w = ws_ref[blk]
        acc = w[0:1, :] * jnp.maximum(rel_ref[:, 0:tq], 0.0)
        for j in range(1, IDX_HEADS):
            acc = acc + w[j:j + 1, :] * jnp.maximum(rel_ref[:, j * tq:(j + 1) * tq], 0.0)
        sc_ref[s] = jnp.where(c * tq + key_in_tile <= jnp.where(blk == 0, t_a, t_b), acc, -jnp.inf)

    issue_rel(0, rel0_ref)

    def score_trip(j, carry):
        s0 = 2 * j
        s1 = jnp.minimum(s0 + 1, nslot - 1)
        issue_rel(s1, rel1_ref)
        score_slot(s0, rel0_ref)
        issue_rel(jnp.minimum(s0 + 2, nslot - 1), rel0_ref)
        score_slot(s1, rel1_ref)
        return carry

    lax.fori_loop(0, (nslot + 1) // 2, score_trip, 0, unroll=True)

    def both_counts(pred):
        acc_a = jnp.zeros((8, tq), F32)
        acc_b = jnp.zeros((8, tq), F32)
        for s in range(nslot):
            is_a = s <= p
            part = _tile_count(pred(s, sc_ref[s], is_a))
            if s == 0:
                acc_a = acc_a + part
            elif s >= nq // 2:
                acc_b = acc_b + part
            else:
                acc_a = acc_a + jnp.where(is_a, part, 0.0)
                acc_b = acc_b + jnp.where(is_a, 0.0, part)
        return jnp.sum(acc_a, axis=0, keepdims=True), jnp.sum(acc_b, axis=0, keepdims=True)

    def value_step(b, carry):
        pre_a, pre_b = carry
        bit = lax.shift_left(jnp.int32(1), 31 - b)
        cand_a, cand_b = pre_a | bit, pre_b | bit
        cf_a, cf_b = _sortable_to_f32(cand_a), _sortable_to_f32(cand_b)
        cnt_a, cnt_b = both_counts(lambda s, tile, is_a: tile >= jnp.where(is_a, cf_a, cf_b))
        return jnp.where(cnt_a >= kf, cand_a, pre_a), jnp.where(cnt_b >= kf, cand_b, pre_b)

    zero_row = jnp.zeros((1, tq), jnp.int32)
    pre_a, pre_b = lax.fori_loop(0, 32, value_step, (zero_row, zero_row), unroll=4)
    thr_a, thr_b = _sortable_to_f32(pre_a), _sortable_to_f32(pre_b)
    above_a, above_b = both_counts(lambda s, tile, is_a: tile > jnp.where(is_a, thr_a, thr_b))
    need_a, need_b = kf - above_a, kf - above_b

    def slot_key(s, is_a):
        return jnp.where(is_a, s, s - n_a) * tq + key_in_tile

    def index_step(b, carry):
        lo_a, lo_b = carry
        bit = lax.shift_left(jnp.int32(1), b)
        cand_a, cand_b = lo_a + bit, lo_b + bit
        cnt_a, cnt_b = both_counts(
            lambda s, tile, is_a: (tile == jnp.where(is_a, thr_a, thr_b))
            & (slot_key(s, is_a) < jnp.where(is_a, cand_a, cand_b)))
        return jnp.where(cnt_a < need_a, cand_a, lo_a), jnp.where(cnt_b < need_b, cand_b, lo_b)

    tie_a, tie_b = both_counts(lambda s, tile, is_a: tile == jnp.where(is_a, thr_a, thr_b))
    excess = jnp.maximum(jnp.max(tie_a - need_a), jnp.max(tie_b - need_b))
    nbits = (nq * tq - 1).bit_length()
    all_keys = jnp.full((1, tq), nq * tq, jnp.int32)
    last_a, last_b = lax.cond(
        excess > 0.0,
        lambda: lax.fori_loop(0, nbits, lambda b, c: index_step(nbits - 1 - b, c), (zero_row, zero_row)),
        lambda: (all_keys, all_keys))

    def finish(t_row, thr, last):
        early = t_row < top_k
        return jnp.where(early, -jnp.finfo(F32).max, thr), jnp.where(early, nq * tq, last)

    thr_a, last_a = finish(t_a, thr_a, last_a)
    thr_b, last_b = finish(t_b, thr_b, last_b)

    n_b = nq - p
    a_slots = n_a + (n_a & 1)

    def att_slot_info(s):
        is_a = s < a_slots
        local = jnp.where(is_a, s, s - a_slots)
        count = jnp.where(is_a, n_a, n_b)
        chunk = jnp.minimum(local, count - 1)
        return jnp.where(is_a, 0, 1), chunk, jnp.where(is_a, chunk, n_a + chunk), local >= count

    for s in range(nslot + 1):
        blk, chunk, src, spare = att_slot_info(s)
        is_a = blk == 0
        thr = jnp.where(is_a, thr_a, thr_b)
        tile = sc_ref[src]
        key = chunk * tq + key_in_tile
        sel = (tile > thr) | ((tile == thr) & (key <= jnp.where(is_a, last_a, last_b)))
        mask_ref[s] = jnp.where(sel & jnp.logical_not(spare), 0.0, MASK_NEG)

    ones_rows = jnp.where(lax.broadcasted_iota(jnp.int32, (16, 2 * tq), 0) == 0, 1.0, 0.0).astype(BF16)
    m_ref[...] = jnp.full(m_ref.shape, -jnp.inf, F32)
    acc_ref[...] = jnp.zeros(acc_ref.shape, F32)

    def issue_qk(pair, g, raw_ref):
        for u in range(2):
            blk, c, _, _ = att_slot_info(2 * pair + u)
            kc = k_ref[pl.ds(pl.multiple_of(c * tq, tq), tq), g * KAUG_W:(g + 1) * KAUG_W]
            raw_ref[u, g] = jnp.dot(kc, qaug_ref[blk * ATT_KV_HEADS + g], preferred_element_type=F32)

    def attend_group(pair, g, raw_ref):
        blk, c0, _, _ = att_slot_info(2 * pair)
        _, c1, _, _ = att_slot_info(2 * pair + 1)
        slot = blk * ATT_KV_HEADS + g
        m_old = m_ref[slot]
        mask = mask_ref[2 * pair]
        logits = jnp.concatenate([raw_ref[0, g, :, r * tq:(r + 1) * tq] + mask for r in range(rep)], axis=1)
        m_mid = jnp.maximum(m_old, jnp.max(logits, axis=0, keepdims=True))
        prob0 = jnp.exp2(logits - m_mid).astype(BF16)
        mask = mask_ref[2 * pair + 1]
        logits = jnp.concatenate([raw_ref[1, g, :, r * tq:(r + 1) * tq] + mask for r in range(rep)], axis=1)
        m_new = jnp.maximum(m_mid, jnp.max(logits, axis=0, keepdims=True))
        prob1 = jnp.exp2(logits - m_new).astype(BF16)
        prob = jnp.concatenate([prob0 * jnp.exp2(m_mid - m_new).astype(BF16), prob1], axis=0)
        v_pair = jnp.concatenate([vt_ref[c0, g * hd:(g + 1) * hd, :], vt_ref[c1, g * hd:(g + 1) * hd, :]], axis=1)
        v_aug = jnp.concatenate([v_pair, ones_rows], axis=0)
        pv = jnp.dot(v_aug, prob, preferred_element_type=F32)
        acc_ref[slot] = acc_ref[slot] * jnp.exp2(m_old - m_new) + pv
        m_ref[slot] = m_new

    npair = (nslot + 1) // 2
    for g in range(ATT_KV_HEADS):
        issue_qk(0, g, raw0_ref)
    for j in range(npair):
        cur, nxt = (raw0_ref, raw1_ref) if j % 2 == 0 else (raw1_ref, raw0_ref)
        for g in range(ATT_KV_HEADS):
            if j + 1 < npair:
                issue_qk(j + 1, g, nxt)
            attend_group(j, g, cur)

    for blk, o_ref in enumerate((oa_ref, ob_ref)):
        outs = []
        for g in range(ATT_KV_HEADS):
            acc = acc_ref[blk * ATT_KV_HEADS + g]
            o_t = acc[:hd, :] / acc[hd:hd + 1, :]
            outs.extend(o_t[:, r * tq:(r + 1) * tq] for r in range(rep))
        o_ref[...] = jnp.concatenate(outs, axis=0).T.astype(o_ref.dtype)


def _dsa(q_t, k, v_t, qi_t, k_idx, wi_t, bsz, seq):
    tq = DSA_QBLOCK
    nq = seq // tq
    half = nq // 2
    top_k = min(TOPK_MAX, seq // 4)
    width = ATT_HEADS * ATT_HEAD_DIM
    gq = (ATT_HEADS // ATT_KV_HEADS) * tq
    col_a = lambda n: pl.BlockSpec((n, tq), lambda b, p: (0, b * nq + p))
    col_b = lambda n: pl.BlockSpec((n, tq), lambda b, p: (0, b * nq + nq - 1 - p))
    full = lambda n: pl.BlockSpec((seq, n), lambda b, p: (b, 0))
    out_a, out_b = pl.pallas_call(
        functools.partial(_dsa_kernel, top_k=top_k, nq=nq),
        out_shape=[jax.ShapeDtypeStruct((bsz, half * tq, width), BF16)] * 2,
        grid=(bsz, half),
        in_specs=[col_a(q_t.shape[0]), col_b(q_t.shape[0]), col_a(qi_t.shape[0]), col_b(qi_t.shape[0]),
                  col_a(wi_t.shape[0]), col_b(wi_t.shape[0]), full(k.shape[1]), full(k_idx.shape[1]),
                  pl.BlockSpec((nq, v_t.shape[1], tq), lambda b, p: (b, 0, 0))],
        out_specs=[pl.BlockSpec((None, tq, width), lambda b, p: (b, p, 0)),
                   pl.BlockSpec((None, tq, width), lambda b, p: (b, half - 1 - p, 0))],
        scratch_shapes=[pltpu.VMEM((nq + 1, tq, tq), F32),
                        pltpu.VMEM((nq + 2, tq, tq), F32),
                        pltpu.VMEM((2 * ATT_KV_HEADS, 1, gq), F32),
                        pltpu.VMEM((2 * ATT_KV_HEADS, ATT_HEAD_DIM + 16, gq), F32),
                        pltpu.VMEM((2 * ATT_KV_HEADS, KAUG_W, gq), BF16),
                        pltpu.VMEM((2, IDX_DIM, IDX_HEADS * tq), BF16),
                        pltpu.VMEM((2, IDX_HEADS, tq), F32),
                        pltpu.VMEM((tq, IDX_HEADS * tq), F32), pltpu.VMEM((tq, IDX_HEADS * tq), F32),
                        pltpu.VMEM((2, ATT_KV_HEADS, tq, gq), F32),
                        pltpu.VMEM((2, ATT_KV_HEADS, tq, gq), F32)],
        compiler_params=_params("parallel", "arbitrary"),
        name="dsa",
    )(q_t, q_t, qi_t, qi_t, wi_t, wi_t, k, k_idx, v_t)
    return jnp.concatenate([out_a, out_b], axis=1).reshape(bsz * seq, width)


def kernel(x, ffn1_norm, ffn1_w_gate, ffn1_w_up, ffn1_w_down, mix_norm, w_in, conv_w, conv_b, dt_bias, a_log,
           d_skip, ssm_norm, w_branch_ssm, w_branch_attn, w_out, ffn2_norm, ffn2_w_gate, ffn2_w_up, ffn2_w_down,
           final_norm):
    bsz, seq, d = x.shape
    depth = ffn1_norm.shape[0]
    h = x.reshape(bsz * seq, d)
    for l in range(depth):
        last = l == depth - 1
        h = _ffn(h, ffn1_norm[l], ffn1_w_gate[l].astype(BF16), ffn1_w_up[l].astype(BF16),
                 ffn1_w_down[l].astype(BF16))
        z, xbc, k, g_ssm, g_att, dt_raw, k_idx, q_t, qi_t, v_t, wi_t = _in_proj(h, mix_norm[l], w_in[l], seq)
        y_ssm = _ssd(xbc, z, dt_raw, conv_w[l], conv_b[l], dt_bias[l], a_log[l], d_skip[l], ssm_norm[l], bsz, seq)
        y_att = _dsa(q_t, k, v_t, qi_t, k_idx, wi_t, bsz, seq)
        merge = (y_ssm, y_att, g_ssm, g_att, w_branch_ssm[l].astype(BF16), w_branch_attn[l].astype(BF16),
                 w_out[l].astype(BF16))
        h = _ffn(h, ffn2_norm[l], ffn2_w_gate[l].astype(BF16), ffn2_w_up[l].astype(BF16),
                 ffn2_w_down[l].astype(BF16), final_g=final_norm if last else None, merge=merge)
    return h.reshape(bsz, seq, d)
```

```python
import functools

import jax
import jax.numpy as jnp
import numpy as np
from jax import lax
from jax.experimental import pallas as pl
from jax.experimental.pallas import tpu as pltpu

F32 = jnp.float32
BF16 = jnp.bfloat16

D_MODEL = 1024
D_INNER = 2048
SSM_HEAD_DIM = 64
SSM_HEADS = 32
SSM_GROUPS = 4
SSM_STATE = 128
CONV_K = 4
XBC_W = D_INNER + 2 * SSM_GROUPS * SSM_STATE
ATT_HEADS = 16
ATT_HEAD_DIM = 64
ATT_KV_HEADS = 4
IDX_HEADS = 8
IDX_DIM = 64
TOPK_MAX = 256
ALIBI_BASE = 8.0
D_FF = 2816
FFN_RESIDUAL_SCALE = 0.5
EPS = 1e-6

IN_SIZES = (D_INNER, XBC_W, SSM_HEADS, ATT_HEADS * ATT_HEAD_DIM, ATT_KV_HEADS * ATT_HEAD_DIM,
            ATT_KV_HEADS * ATT_HEAD_DIM, IDX_HEADS * IDX_DIM, IDX_DIM, IDX_HEADS, D_MODEL, D_MODEL)

VMEM_LIMIT_BYTES = 56 * 1024 * 1024
LOG2E = 1.4426950408889634
SSD_CHUNK = 128
SSD_SUB = 4
CONV_HALO = 16
SSD_CONV_COLS = 512
DSA_QBLOCK = 128
KAUG_W = 128
FFN_ROWS = 512
FFN_FCHUNK = 256
PROJ_ROWS = 512
MASK_NEG = -1e30


def _rms(x, g):
    ms = jnp.mean(x * x, axis=-1, keepdims=True)
    return x * lax.rsqrt(ms + EPS) * g


def _silu(x):
    return x / (1.0 + jnp.exp2(x * -LOG2E))


def _const_spec(shape):
    return pl.BlockSpec(shape, lambda *_: (0,) * len(shape), pipeline_mode=pl.Buffered(1))


def _params(*sem):
    return pltpu.CompilerParams(dimension_semantics=sem, vmem_limit_bytes=VMEM_LIMIT_BYTES)


def _ffn_kernel(x_ref, *rest, merge, final_norm):
    rest = list(rest)
    x = x_ref[...]
    if merge:
        ya_ref, yb_ref, ga_ref, gb_ref, wa_ref, wb_ref, wo_ref = rest[:7]
        rest = rest[7:]
        pa = jnp.dot(ya_ref[...], wa_ref[...], preferred_element_type=F32)
        pb = jnp.dot(yb_ref[...], wb_ref[...], preferred_element_type=F32)
        merged = jax.nn.sigmoid(ga_ref[...].astype(F32)) * pa + jax.nn.sigmoid(gb_ref[...].astype(F32)) * pb
        x = x + jnp.dot(merged.astype(BF16), wo_ref[...], preferred_element_type=F32)
    g_ref, wg_ref, wu_ref, wd_ref = rest[:4]
    rest = rest[4:]
    if final_norm:
        fg_ref = rest.pop(0)
    o_ref, acc_ref = rest
    xn = _rms(x, g_ref[...]).astype(BF16)
    d_ff = wg_ref.shape[1]
    for c in range(0, d_ff, FFN_FCHUNK):
        gate = jnp.dot(xn, wg_ref[:, c:c + FFN_FCHUNK], preferred_element_type=F32)
        up = jnp.dot(xn, wu_ref[:, c:c + FFN_FCHUNK], preferred_element_type=F32)
        act = (gate * jax.nn.sigmoid(gate) * up).astype(BF16)
        part = jnp.dot(act, wd_ref[c:c + FFN_FCHUNK, :], preferred_element_type=F32)
        if c == 0:
            acc_ref[...] = part
        else:
            acc_ref[...] += part
    y = x + FFN_RESIDUAL_SCALE * acc_ref[...]
    if final_norm:
        y = _rms(y, fg_ref[...])
    o_ref[...] = y


def _ffn(x, norm_g, wg, wu, wd, final_g=None, merge=None):
    t, d = x.shape
    d_ff = wg.shape[1]
    rows = min(FFN_ROWS, t)
    row_spec = lambda n: pl.BlockSpec((rows, n), lambda i: (i, 0))
    in_specs = [row_spec(d)]
    args = [x]
    if merge is not None:
        in_specs += [row_spec(a.shape[1]) for a in merge[:4]] + [_const_spec(w.shape) for w in merge[4:]]
        args += list(merge)
    in_specs += [_const_spec((1, d)), _const_spec((d, d_ff)), _const_spec((d, d_ff)), _const_spec((d_ff, d))]
    args += [norm_g.reshape(1, d), wg, wu, wd]
    if final_g is not None:
        in_specs.append(_const_spec((1, d)))
        args.append(final_g.reshape(1, d))
    return pl.pallas_call(
        functools.partial(_ffn_kernel, merge=merge is not None, final_norm=final_g is not None),
        out_shape=jax.ShapeDtypeStruct((t, d), F32),
        grid=(t // rows,),
        in_specs=in_specs,
        out_specs=pl.BlockSpec((rows, d), lambda i: (i, 0)),
        scratch_shapes=[pltpu.VMEM((rows, d), F32)],
        compiler_params=_params("parallel"),
        name="merge_ffn" if merge is not None else "ffn",
    )(*args)


def _dot_nt(a, b):
    return lax.dot_general(a, b, (((1,), (1,)), ((), ())), preferred_element_type=F32)


def _in_proj_kernel(x_ref, g_ref, wz, wxbc, wk, wgs, wga, wmisc, wqt, wqit, wvt, wwit,
                    oz, oxbc, ok, ogs, oga, odt, oki, oqt, oqit, ovt, owit, *, seq):
    xn = _rms(x_ref[...], g_ref[...]).astype(BF16)
    for w_ref, o_ref in ((wz, oz), (wxbc, oxbc), (wgs, ogs), (wga, oga)):
        n = w_ref.shape[1]
        step = min(n, 512)
        for c in range(0, n, step):
            o_ref[:, c:c + step] = jnp.dot(xn, w_ref[:, c:c + step], preferred_element_type=F32).astype(o_ref.dtype)
    rows = x_ref.shape[0]
    pos = (pl.program_id(0) * rows) % seq + lax.broadcasted_iota(jnp.int32, (rows, KAUG_W), 0)
    lane = lax.broadcasted_iota(jnp.int32, (rows, KAUG_W), 1) - ATT_HEAD_DIM
    chunk_f = lax.shift_right_logical(pos, DSA_QBLOCK.bit_length() - 1).astype(F32)
    row_f = (pos & (DSA_QBLOCK - 1)).astype(F32)
    pos_cols = jnp.where((lane >= 0) & (lane < 3), chunk_f,
                         jnp.where((lane >= 3) & (lane < 6), row_f,
                                   jnp.where((lane >= 6) & (lane < 9), 1.0, 0.0)))
    k_nat = jnp.dot(xn, wk[...], preferred_element_type=F32) * LOG2E
    zeros = jnp.zeros((rows, KAUG_W - ATT_HEAD_DIM), F32)
    for g in range(ATT_KV_HEADS):
        k_g = jnp.concatenate([k_nat[:, g * ATT_HEAD_DIM:(g + 1) * ATT_HEAD_DIM], zeros], axis=1)
        ok[:, g * KAUG_W:(g + 1) * KAUG_W] = (k_g + pos_cols).astype(BF16)
    misc = jnp.dot(xn, wmisc[...], preferred_element_type=F32)
    odt[...] = misc[:, :SSM_HEADS]
    oki[...] = misc[:, SSM_HEADS:SSM_HEADS + IDX_DIM].astype(BF16)
    for w_ref, o_ref in ((wqt, oqt), (wqit, oqit)):
        n = w_ref.shape[0]
        for c in range(0, n, 256):
            o_ref[c:c + 256, :] = _dot_nt(w_ref[c:c + 256, :], xn).astype(o_ref.dtype)
    owit[...] = _dot_nt(wwit[...], xn)[:IDX_HEADS, :]
    vt = _dot_nt(wvt[...], xn).astype(BF16)
    for j in range(ovt.shape[0]):
        ovt[j] = vt[:, j * DSA_QBLOCK:(j + 1) * DSA_QBLOCK]


def _in_proj(x1, norm_g, w_in, seq):
    t, d = x1.shape
    offs = [0]
    for s in IN_SIZES:
        offs.append(offs[-1] + s)
    seg = lambda i: w_in[:, offs[i]:offs[i + 1]]
    w_misc = jnp.concatenate([seg(2), seg(7), jnp.zeros((d, 128 - SSM_HEADS - IDX_DIM), w_in.dtype)], axis=1)
    natural = [seg(0), seg(1), seg(4), seg(9), seg(10), w_misc]
    w_idx_t = jnp.concatenate([seg(8).T, jnp.zeros((16 - IDX_HEADS, d), w_in.dtype)], axis=0)
    transposed = [seg(3).T, seg(6).T, seg(5).T, w_idx_t]
    weights = [w.astype(BF16) for w in natural + transposed]
    rows = min(PROJ_ROWS, t)
    nchunk = rows // DSA_QBLOCK
    row_spec = lambda n: pl.BlockSpec((rows, n), lambda i: (i, 0))
    col_spec = lambda n: pl.BlockSpec((n, rows), lambda i: (0, i))
    kv_w = ATT_KV_HEADS * ATT_HEAD_DIM
    widths = [IN_SIZES[0], IN_SIZES[1], ATT_KV_HEADS * KAUG_W, IN_SIZES[9], IN_SIZES[10]]
    out_shapes = [jax.ShapeDtypeStruct((t, n), BF16) for n in widths]
    out_shapes += [jax.ShapeDtypeStruct((t, SSM_HEADS), F32), jax.ShapeDtypeStruct((t, IDX_DIM), BF16),
                   jax.ShapeDtypeStruct((ATT_HEADS * ATT_HEAD_DIM, t), BF16),
                   jax.ShapeDtypeStruct((IDX_HEADS * IDX_DIM, t), BF16),
                   jax.ShapeDtypeStruct((t // DSA_QBLOCK, kv_w, DSA_QBLOCK), BF16),
                   jax.ShapeDtypeStruct((IDX_HEADS, t), F32)]
    out_specs = [row_spec(s.shape[1]) for s in out_shapes[:7]]
    out_specs += [col_spec(ATT_HEADS * ATT_HEAD_DIM), col_spec(IDX_HEADS * IDX_DIM),
                  pl.BlockSpec((nchunk, kv_w, DSA_QBLOCK), lambda i: (i, 0, 0)), col_spec(IDX_HEADS)]
    return pl.pallas_call(
        functools.partial(_in_proj_kernel, seq=seq),
        out_shape=out_shapes,
        grid=(t // rows,),
        in_specs=[row_spec(d), _const_spec((1, d))] + [_const_spec(w.shape) for w in weights],
        out_specs=out_specs,
        compiler_params=_params("parallel"),
        name="in_proj",
    )(x1, norm_g.reshape(1, d), *weights)


def _ssd_kernel(xbc_ref, z_ref, dt_ref, shift_ref, convw_ref, convb_ref, dtb_ref, ahead2_ref, dexp_ref, normw_ref,
                o_ref, win_ref, state_ref):
    @pl.when(pl.program_id(1) == 0)
    def _():
        win_ref[0:CONV_HALO, :] = jnp.zeros((CONV_HALO, XBC_W), BF16)
        state_ref[...] = jnp.zeros_like(state_ref)

    for sub in range(xbc_ref.shape[0] // SSD_CHUNK):
        _ssd_chunk(slice(sub * SSD_CHUNK, (sub + 1) * SSD_CHUNK), xbc_ref, z_ref, dt_ref, shift_ref, convw_ref,
                   convb_ref, dtb_ref, ahead2_ref, dexp_ref, normw_ref, o_ref, win_ref, state_ref)


def _ssd_chunk(rows, xbc_ref, z_ref, dt_ref, shift_ref, convw_ref, convb_ref, dtb_ref, ahead2_ref, dexp_ref,
               normw_ref, o_ref, win_ref, state_ref):
    q = SSD_CHUNK
    hd, n, g = SSM_HEAD_DIM, SSM_STATE, SSM_GROUPS
    heads = SSM_HEADS
    hpg = heads // g
    gw = hpg * hd

    x_cur = xbc_ref[rows, :]
    win_ref[CONV_HALO:CONV_HALO + q, :] = x_cur
    pieces = []
    for c0 in range(0, XBC_W, SSD_CONV_COLS):
        cols = slice(c0, c0 + SSD_CONV_COLS)
        shifted = jnp.dot(shift_ref[...], win_ref[:, cols], preferred_element_type=F32)
        conv = convb_ref[:, cols] + convw_ref[CONV_K - 1:CONV_K, cols] * x_cur[:, cols].astype(F32)
        for k in range(CONV_K - 1):
            conv = conv + convw_ref[k:k + 1, cols] * shifted[k * q:(k + 1) * q, :]
        pieces.append(_silu(conv))
    win_ref[0:CONV_HALO, :] = x_cur[q - CONV_HALO:, :]
    xbc = jnp.concatenate(pieces, axis=1)
    xs = xbc[:, :D_INNER]
    bm = xbc[:, D_INNER:D_INNER + g * n]
    cm = xbc[:, D_INNER + g * n:]

    dt = jax.nn.softplus(dt_ref[rows, :] + dtb_ref[...])
    a2 = dt * ahead2_ref[...]
    row = lax.broadcasted_iota(jnp.int32, (q, q), 0)
    col = lax.broadcasted_iota(jnp.int32, (q, q), 1)
    causal = row >= col
    tril = jnp.where(causal, 1.0, 0.0).astype(BF16)
    a2_cs = _split_dot_left(tril, a2)
    a2_cs_t = _transpose_heads(a2_cs)
    a2_last = a2_cs[q - 1:q, :]
    stack = jnp.concatenate([dt, jnp.exp2(a2_cs), dt * jnp.exp2(a2_last - a2_cs),
                             jnp.broadcast_to(jnp.exp2(a2_last), (16, heads))], axis=0)
    hi = stack.astype(BF16)
    lo = (stack - hi.astype(F32)).astype(BF16)
    lane = lax.broadcasted_iota(jnp.int32, (2 * heads, heads * hd), 1)
    src = lax.broadcasted_iota(jnp.int32, (2 * heads, heads * hd), 0)
    first = jnp.where(src >= heads, src - heads, src) * hd
    expand = jnp.where((lane >= first) & (lane < first + hd), 1.0, 0.0).astype(BF16)
    ex = jnp.dot(jnp.concatenate([hi, lo], axis=1), expand, preferred_element_type=F32)
    dt_x, in_decay_x, w_state_x, chunk_decay_x = ex[0:q], ex[q:2 * q], ex[2 * q:3 * q], ex[3 * q:3 * q + 1]

    prev = state_ref[...]
    prev_b = prev.astype(BF16)
    lane_pair = lax.broadcasted_iota(jnp.int32, (q, 2 * hd), 1)
    ys = []
    for gi in range(g):
        sl = slice(gi * gw, (gi + 1) * gw)
        bg = bm[:, gi * n:(gi + 1) * n]
        bg_b = bg.astype(BF16)
        cg_b = cm[:, gi * n:(gi + 1) * n].astype(BF16)
        cb = jnp.where(causal, _dot_nt(cg_b, bg_b), 0.0)
        xs_g = xs[:, sl]
        xdt_b = (xs_g * dt_x[:, sl]).astype(BF16)
        y_off = jnp.dot(cg_b, prev_b[:, sl], preferred_element_type=F32)
        pairs = []
        for pr in range(hpg // 2):
            ms = []
            for h in (gi * hpg + 2 * pr, gi * hpg + 2 * pr + 1):
                seg = jnp.broadcast_to(a2_cs[:, h:h + 1], (q, q)) - a2_cs_t[h:h + 1, :]
                ms.append((cb * jnp.exp2(jnp.minimum(seg, 0.0))).astype(BF16))
            xp = xdt_b[:, 2 * pr * hd:(2 * pr + 2) * hd]
            zero = jnp.zeros_like(xp)
            rhs = jnp.concatenate([jnp.where(lane_pair < hd, xp, zero), jnp.where(lane_pair >= hd, xp, zero)], axis=0)
            pairs.append(jnp.dot(jnp.concatenate(ms, axis=1), rhs, preferred_element_type=F32))
        ys.append(jnp.concatenate(pairs, axis=1) + y_off * in_decay_x[:, sl] + dexp_ref[:, sl] * xs_g)
        xw = (xs_g * w_state_x[:, sl]).astype(BF16)
        s_new = jnp.dot(bg.T.astype(BF16), xw, preferred_element_type=F32)
        state_ref[:, sl] = prev[:, sl] * chunk_decay_x[:, sl] + s_new
    y = jnp.concatenate(ys, axis=1)

    z = z_ref[rows, :].astype(F32)
    yg = y * _silu(z)
    outs = []
    for gi in range(g):
        blk = yg[:, gi * gw:(gi + 1) * gw]
        ms = jnp.mean(blk * blk, axis=-1, keepdims=True)
        outs.append(blk * lax.rsqrt(ms + EPS))
    o_ref[rows, :] = (jnp.concatenate(outs, axis=1) * normw_ref[...]).astype(o_ref.dtype)


def _transpose_heads(x):
    q, h = x.shape
    return jnp.concatenate([x, jnp.zeros((q, q - h), x.dtype)], axis=1).T[:h, :]


def _split_dot_left(b01, a):
    out = None
    rem = a
    for _ in range(3):
        hi = rem.astype(BF16)
        term = jnp.dot(b01, hi, preferred_element_type=F32)
        out = term if out is None else out + term
        rem = rem - hi.astype(F32)
    return out


def _conv_shift_matrix(q):
    s = np.zeros(((CONV_K - 1) * q, CONV_HALO + q), np.float32)
    for k in range(CONV_K - 1):
        s[k * q + np.arange(q), CONV_HALO + np.arange(q) - (CONV_K - 1) + k] = 1.0
    return jnp.asarray(s, BF16)


def _ssd(xbc, z, dt_raw, conv_w, conv_b, dt_bias, a_log, d_skip, norm_w, bsz, seq):
    q = SSD_CHUNK
    step_rows = q * min(SSD_SUB, seq // q)
    nc = seq // step_rows
    a_head2 = (-jnp.exp(a_log.astype(F32)) * LOG2E).reshape(1, SSM_HEADS)
    d_exp = jnp.repeat(d_skip.astype(F32), SSM_HEAD_DIM).reshape(1, D_INNER)
    shift = _conv_shift_matrix(q)
    row_spec = lambda n: pl.BlockSpec((step_rows, n), lambda b, c: (b * nc + c, 0))
    return pl.pallas_call(
        _ssd_kernel,
        out_shape=jax.ShapeDtypeStruct((bsz * seq, D_INNER), BF16),
        grid=(bsz, nc),
        in_specs=[row_spec(XBC_W), row_spec(D_INNER), row_spec(SSM_HEADS), _const_spec(shift.shape),
                  _const_spec((CONV_K, XBC_W)), _const_spec((1, XBC_W)), _const_spec((1, SSM_HEADS)),
                  _const_spec((1, SSM_HEADS)), _const_spec((1, D_INNER)), _const_spec((1, D_INNER))],
        out_specs=row_spec(D_INNER),
        scratch_shapes=[pltpu.VMEM((CONV_HALO + q, XBC_W), BF16), pltpu.VMEM((SSM_STATE, D_INNER), F32)],
        compiler_params=_params("parallel", "arbitrary"),
        name="ssd",
    )(xbc, z, dt_raw, shift, conv_w.astype(F32), conv_b.reshape(1, XBC_W).astype(F32),
      dt_bias.reshape(1, SSM_HEADS).astype(F32), a_head2, d_exp, norm_w.reshape(1, D_INNER).astype(F32))


def _sortable_to_f32(s):
    bits = jnp.where(s < 0, s ^ jnp.int32(-2 ** 31), ~s)
    return lax.bitcast_convert_type(bits, F32)


def _tile_count(mask):
    ones = jnp.where(mask, 1.0, 0.0)
    return jnp.sum(ones.reshape(ones.shape[0] // 8, 8, ones.shape[1]), axis=0)


def _np_bf16_pieces(x, n=3):
    out = []
    rem = np.float32(x)
    for _ in range(n):
        piece = np.float32(np.asarray(rem, dtype=jnp.bfloat16))
        out.append(piece)
        rem = np.float32(rem - piece)
    return out


def _bf16_pieces(x, n=3):
    out = []
    rem = x
    for _ in range(n):
        piece = rem.astype(BF16).astype(F32)
        out.append(piece)
        rem = rem - piece
    return out


def _dsa_kernel(qa_ref, qb_ref, qia_ref, qib_ref, wa_ref, wb_ref, k_ref, ki_ref, vt_ref, oa_ref, ob_ref,
                sc_ref, mask_ref, m_ref, acc_ref, qaug_ref, qis_ref, ws_ref, rel0_ref, rel1_ref, raw0_ref, raw1_ref,
                *, top_k, nq):
    tq = DSA_QBLOCK
    rep = ATT_HEADS // ATT_KV_HEADS
    hd = ATT_HEAD_DIM
    p = pl.program_id(1)
    n_a = p + 1
    nslot = nq + 1
    t_a = p * tq + lax.broadcasted_iota(jnp.int32, (1, tq), 1)
    t_b = (nq - 1 - p) * tq + lax.broadcasted_iota(jnp.int32, (1, tq), 1)
    key_in_tile = lax.broadcasted_iota(jnp.int32, (tq, tq), 0)
    kf = float(top_k)

    def slot_info(s):
        is_a = s <= p
        return jnp.where(is_a, 0, 1), jnp.minimum(jnp.where(is_a, s, s - n_a), nq - 1)

    for blk, (qi_ref, w_ref, q_ref, t_row) in enumerate(((qia_ref, wa_ref, qa_ref, t_a), (qib_ref, wb_ref, qb_ref, t_b))):
        qi = qi_ref[...]
        qis_ref[blk] = jnp.concatenate([qi[j * IDX_DIM:(j + 1) * IDX_DIM, :] for j in range(IDX_HEADS)], axis=1)
        ws_ref[blk] = w_ref[...] * (IDX_HEADS ** -0.5) * (IDX_DIM ** -0.5)
        row16 = lax.broadcasted_iota(jnp.int32, (16, tq), 0)
        for g in range(ATT_KV_HEADS):
            q_t = jnp.concatenate([q_ref[(g * rep + r) * hd:(g * rep + r + 1) * hd, :] for r in range(rep)], axis=1)
            bias_rows = []
            for r in range(rep):
                slope = np.float32(2.0 ** (-ALIBI_BASE * (g * rep + r + 1) / ATT_HEADS) * LOG2E)
                pieces = [float(v) for v in _np_bf16_pieces(slope)]
                terms = [tq * v for v in pieces] + pieces + _bf16_pieces(-(slope * t_row.astype(F32)))
                rows = jnp.zeros((16, tq), F32)
                for i, term in enumerate(terms):
                    rows = jnp.where(row16 == i, term, rows)
                bias_rows.append(rows)
            qaug_ref[blk * ATT_KV_HEADS + g] = jnp.concatenate(
                [q_t * (hd ** -0.5), jnp.concatenate(bias_rows, axis=1).astype(BF16),
                 jnp.zeros((KAUG_W - hd - 16, rep * tq), BF16)], axis=0)

    def issue_rel(s, rel_ref):
        blk, c = slot_info(s)
        kc = ki_ref[pl.ds(pl.multiple_of(c * tq, tq), tq), :]
        rel_ref[...] = jnp.dot(kc, qis_ref[blk], preferred_element_type=F32)

    def score_slot(s, rel_ref):
        blk, c = slot_info(s)
        w = ws_ref[blk]
        acc = w[0:1, :] * jnp.maximum(rel_ref[:, 0:tq], 0.0)
        for j in range(1, IDX_HEADS):
            acc = acc + w[j:j + 1, :] * jnp.maximum(rel_ref[:, j * tq:(j + 1) * tq], 0.0)
        sc_ref[s] = jnp.where(c * tq + key_in_tile <= jnp.where(blk == 0, t_a, t_b), acc, -jnp.inf)

    issue_rel(0, rel0_ref)

    def score_trip(j, carry):
        s0 = 2 * j
        s1 = jnp.minimum(s0 + 1, nslot - 1)
        issue_rel(s1, rel1_ref)
        score_slot(s0, rel0_ref)
        issue_rel(jnp.minimum(s0 + 2, nslot - 1), rel0_ref)
        score_slot(s1, rel1_ref)
        return carry

    lax.fori_loop(0, (nslot + 1) // 2, score_trip, 0, unroll=True)

    def both_counts(pred):
        acc_a = jnp.zeros((8, tq), F32)
        acc_b = jnp.zeros((8, tq), F32)
        for s in range(nslot):
            is_a = s <= p
            part = _tile_count(pred(s, sc_ref[s], is_a))
            if s == 0:
                acc_a = acc_a + part
            elif s >= nq // 2:
                acc_b = acc_b + part
            else:
                acc_a = acc_a + jnp.where(is_a, part, 0.0)
                acc_b = acc_b + jnp.where(is_a, 0.0, part)
        return jnp.sum(acc_a, axis=0, keepdims=True), jnp.sum(acc_b, axis=0, keepdims=True)

    def value_step(b, carry):
        pre_a, pre_b = carry
        bit = lax.shift_left(jnp.int32(1), 31 - b)
        cand_a, cand_b = pre_a | bit, pre_b | bit
        cf_a, cf_b = _sortable_to_f32(cand_a), _sortable_to_f32(cand_b)
        cnt_a, cnt_b = both_counts(lambda s, tile, is_a: tile >= jnp.where(is_a, cf_a, cf_b))
        return jnp.where(cnt_a >= kf, cand_a, pre_a), jnp.where(cnt_b >= kf, cand_b, pre_b)

    zero_row = jnp.zeros((1, tq), jnp.int32)
    pre_a, pre_b = lax.fori_loop(0, 32, value_step, (zero_row, zero_row), unroll=4)
    thr_a, thr_b = _sortable_to_f32(pre_a), _sortable_to_f32(pre_b)
    above_a, above_b = both_counts(lambda s, tile, is_a: tile > jnp.where(is_a, thr_a, thr_b))
    need_a, need_b = kf - above_a, kf - above_b

    def slot_key(s, is_a):
        return jnp.where(is_a, s, s - n_a) * tq + key_in_tile

    def index_step(b, carry):
        lo_a, lo_b = carry
        bit = lax.shift_left(jnp.int32(1), b)
        cand_a, cand_b = lo_a + bit, lo_b + bit
        cnt_a, cnt_b = both_counts(
            lambda s, tile, is_a: (tile == jnp.where(is_a, thr_a, thr_b))
            & (slot_key(s, is_a) < jnp.where(is_a, cand_a, cand_b)))
        return jnp.where(cnt_a < need_a, cand_a, lo_a), jnp.where(cnt_b < need_b, cand_b, lo_b)

    tie_a, tie_b = both_counts(lambda s, tile, is_a: tile == jnp.where(is_a, thr_a, thr_b))
    excess = jnp.maximum(jnp.max(tie_a - need_a), jnp.max(tie_b - need_b))
    nbits = (nq * tq - 1).bit_length()
    all_keys = jnp.full((1, tq), nq * tq, jnp.int32)
    last_a, last_b = lax.cond(
        excess > 0.0,
        lambda: lax.fori_loop(0, nbits, lambda b, c: index_step(nbits - 1 - b, c), (zero_row, zero_row)),
        lambda: (all_keys, all_keys))

    def finish(t_row, thr, last):
        early = t_row < top_k
        return jnp.where(early, -jnp.finfo(F32).max, thr), jnp.where(early, nq * tq, last)

    thr_a, last_a = finish(t_a, thr_a, last_a)
    thr_b, last_b = finish(t_b, thr_b, last_b)

    n_b = nq - p
    a_slots = n_a + (n_a & 1)

    def att_slot_info(s):
        is_a = s < a_slots
        local = jnp.where(is_a, s, s - a_slots)
        count = jnp.where(is_a, n_a, n_b)
        chunk = jnp.minimum(local, count - 1)
        return jnp.where(is_a, 0, 1), chunk, jnp.where(is_a, chunk, n_a + chunk), local >= count

    for s in range(nslot + 1):
        blk, chunk, src, spare = att_slot_info(s)
        is_a = blk == 0
        thr = jnp.where(is_a, thr_a, thr_b)
        tile = sc_ref[src]
        key = chunk * tq + key_in_tile
        sel = (tile > thr) | ((tile == thr) & (key <= jnp.where(is_a, last_a, last_b)))
        mask_ref[s] = jnp.where(sel & jnp.logical_not(spare), 0.0, MASK_NEG)

    ones_rows = jnp.where(lax.broadcasted_iota(jnp.int32, (16, 2 * tq), 0) == 0, 1.0, 0.0).astype(BF16)
    m_ref[...] = jnp.full(m_ref.shape, -jnp.inf, F32)
    acc_ref[...] = jnp.zeros(acc_ref.shape, F32)

    def issue_qk(pair, g, raw_ref):
        for u in range(2):
            blk, c, _, _ = att_slot_info(2 * pair + u)
            kc = k_ref[pl.ds(pl.multiple_of(c * tq, tq), tq), g * KAUG_W:(g + 1) * KAUG_W]
            raw_ref[u, g] = jnp.dot(kc, qaug_ref[blk * ATT_KV_HEADS + g], preferred_element_type=F32)

    def attend_group(pair, g, raw_ref):
        blk, c0, _, _ = att_slot_info(2 * pair)
        _, c1, _, _ = att_slot_info(2 * pair + 1)
        slot = blk * ATT_KV_HEADS + g
        m_old = m_ref[slot]
        mask = mask_ref[2 * pair]
        logits = jnp.concatenate([raw_ref[0, g, :, r * tq:(r + 1) * tq] + mask for r in range(rep)], axis=1)
        m_mid = jnp.maximum(m_old, jnp.max(logits, axis=0, keepdims=True))
        prob0 = jnp.exp2(logits - m_mid).astype(BF16)
        mask = mask_ref[2 * pair + 1]
        logits = jnp.concatenate([raw_ref[1, g, :, r * tq:(r + 1) * tq] + mask for r in range(rep)], axis=1)
        m_new = jnp.maximum(m_mid, jnp.max(logits, axis=0, keepdims=True))
        prob1 = jnp.exp2(logits - m_new).astype(BF16)
        prob = jnp.concatenate([prob0 * jnp.exp2(m_mid - m_new).astype(BF16), prob1], axis=0)
        v_pair = jnp.concatenate([vt_ref[c0, g * hd:(g + 1) * hd, :], vt_ref[c1, g * hd:(g + 1) * hd, :]], axis=1)
        v_aug = jnp.concatenate([v_pair, ones_rows], axis=0)
        pv = jnp.dot(v_aug, prob, preferred_element_type=F32)
        acc_ref[slot] = acc_ref[slot] * jnp.exp2(m_old - m_new) + pv
        m_ref[slot] = m_new

    npair = (nslot + 1) // 2
    for g in range(ATT_KV_HEADS):
        issue_qk(0, g, raw0_ref)
    for j in range(npair):
        cur, nxt = (raw0_ref, raw1_ref) if j % 2 == 0 else (raw1_ref, raw0_ref)
        for g in range(ATT_KV_HEADS):
            if j + 1 < npair:
                issue_qk(j + 1, g, nxt)
            attend_group(j, g, cur)

    for blk, o_ref in enumerate((oa_ref, ob_ref)):
        outs = []
        for g in range(ATT_KV_HEADS):
            acc = acc_ref[blk * ATT_KV_HEADS + g]
            o_t = acc[:hd, :] / acc[hd:hd + 1, :]
            outs.extend(o_t[:, r * tq:(r + 1) * tq] for r in range(rep))
        o_ref[...] = jnp.concatenate(outs, axis=0).T.astype(o_ref.dtype)


def _dsa(q_t, k, v_t, qi_t, k_idx, wi_t, bsz, seq):
    tq = DSA_QBLOCK
    nq = seq // tq
    half = nq // 2
    top_k = min(TOPK_MAX, seq // 4)
    width = ATT_HEADS * ATT_HEAD_DIM
    gq = (ATT_HEADS // ATT_KV_HEADS) * tq
    col_a = lambda n: pl.BlockSpec((n, tq), lambda b, p: (0, b * nq + p))
    col_b = lambda n: pl.BlockSpec((n, tq), lambda b, p: (0, b * nq + nq - 1 - p))
    full = lambda n: pl.BlockSpec((seq, n), lambda b, p: (b, 0))
    out_a, out_b = pl.pallas_call(
        functools.partial(_dsa_kernel, top_k=top_k, nq=nq),
        out_shape=[jax.ShapeDtypeStruct((bsz, half * tq, width), BF16)] * 2,
        grid=(bsz, half),
        in_specs=[col_a(q_t.shape[0]), col_b(q_t.shape[0]), col_a(qi_t.shape[0]), col_b(qi_t.shape[0]),
                  col_a(wi_t.shape[0]), col_b(wi_t.shape[0]), full(k.shape[1]), full(k_idx.shape[1]),
                  pl.BlockSpec((nq, v_t.shape[1], tq), lambda b, p: (b, 0, 0))],
        out_specs=[pl.BlockSpec((None, tq, width), lambda b, p: (b, p, 0)),
                   pl.BlockSpec((None, tq, width), lambda b, p: (b, half - 1 - p, 0))],
        scratch_shapes=[pltpu.VMEM((nq + 1, tq, tq), F32),
                        pltpu.VMEM((nq + 2, tq, tq), F32),
                        pltpu.VMEM((2 * ATT_KV_HEADS, 1, gq), F32),
                        pltpu.VMEM((2 * ATT_KV_HEADS, ATT_HEAD_DIM + 16, gq), F32),
                        pltpu.VMEM((2 * ATT_KV_HEADS, KAUG_W, gq), BF16),
                        pltpu.VMEM((2, IDX_DIM, IDX_HEADS * tq), BF16),
                        pltpu.VMEM((2, IDX_HEADS, tq), F32),
                        pltpu.VMEM((tq, IDX_HEADS * tq), F32), pltpu.VMEM((tq, IDX_HEADS * tq), F32),
                        pltpu.VMEM((2, ATT_KV_HEADS, tq, gq), F32),
                        pltpu.VMEM((2, ATT_KV_HEADS, tq, gq), F32)],
        compiler_params=_params("parallel", "arbitrary"),
        name="dsa",
    )(q_t, q_t, qi_t, qi_t, wi_t, wi_t, k, k_idx, v_t)
    return jnp.concatenate([out_a, out_b], axis=1).reshape(bsz * seq, width)


def kernel(x, ffn1_norm, ffn1_w_gate, ffn1_w_up, ffn1_w_down, mix_norm, w_in, conv_w, conv_b, dt_bias, a_log,
           d_skip, ssm_norm, w_branch_ssm, w_branch_attn, w_out, ffn2_norm, ffn2_w_gate, ffn2_w_up, ffn2_w_down,
           final_norm):
    bsz, seq, d = x.shape
    depth = ffn1_norm.shape[0]
    h = x.reshape(bsz * seq, d)
    for l in range(depth):
        last = l == depth - 1
        h = _ffn(h, ffn1_norm[l], ffn1_w_gate[l].astype(BF16), ffn1_w_up[l].astype(BF16),
                 ffn1_w_down[l].astype(BF16))
        z, xbc, k, g_ssm, g_att, dt_raw, k_idx, q_t, qi_t, v_t, wi_t = _in_proj(h, mix_norm[l], w_in[l], seq)
        y_ssm = _ssd(xbc, z, dt_raw, conv_w[l], conv_b[l], dt_bias[l], a_log[l], d_skip[l], ssm_norm[l], bsz, seq)
        y_att = _dsa(q_t, k, v_t, qi_t, k_idx, wi_t, bsz, seq)
        merge = (y_ssm, y_att, g_ssm, g_att, w_branch_ssm[l].astype(BF16), w_branch_attn[l].astype(BF16),
                 w_out[l].astype(BF16))
        h = _ffn(h, ffn2_norm[l], ffn2_w_gate[l].astype(BF16), ffn2_w_up[l].astype(BF16),
                 ffn2_w_down[l].astype(BF16), final_g=final_norm if last else None, merge=merge)
    return h.reshape(bsz, seq, d)
```

```python
import functools

import jax
import jax.numpy as jnp
import numpy as np
from jax import lax
from jax.experimental import pallas as pl
from jax.experimental.pallas import tpu as pltpu

F32 = jnp.float32
BF16 = jnp.bfloat16

D_MODEL = 1024
D_INNER = 2048
SSM_HEAD_DIM = 64
SSM_HEADS = 32
SSM_GROUPS = 4
SSM_STATE = 128
CONV_K = 4
XBC_W = D_INNER + 2 * SSM_GROUPS * SSM_STATE
ATT_HEADS = 16
ATT_HEAD_DIM = 64
ATT_KV_HEADS = 4
IDX_HEADS = 8
IDX_DIM = 64
TOPK_MAX = 256
ALIBI_BASE = 8.0
D_FF = 2816
FFN_RESIDUAL_SCALE = 0.5
EPS = 1e-6

IN_SIZES = (D_INNER, XBC_W, SSM_HEADS, ATT_HEADS * ATT_HEAD_DIM, ATT_KV_HEADS * ATT_HEAD_DIM,
            ATT_KV_HEADS * ATT_HEAD_DIM, IDX_HEADS * IDX_DIM, IDX_DIM, IDX_HEADS, D_MODEL, D_MODEL)

VMEM_LIMIT_BYTES = 56 * 1024 * 1024
LOG2E = 1.4426950408889634
SSD_CHUNK = 128
SSD_SUB = 4
CONV_HALO = 16
SSD_CONV_COLS = 512
DSA_QBLOCK = 128
KAUG_W = 128
FFN_ROWS = 512
FFN_FCHUNK = 256
PROJ_ROWS = 512
MASK_NEG = -1e30


def _rms(x, g):
    ms = jnp.mean(x * x, axis=-1, keepdims=True)
    return x * lax.rsqrt(ms + EPS) * g


def _silu(x):
    return x / (1.0 + jnp.exp2(x * -LOG2E))


def _const_spec(shape):
    return pl.BlockSpec(shape, lambda *_: (0,) * len(shape), pipeline_mode=pl.Buffered(1))


def _params(*sem):
    return pltpu.CompilerParams(dimension_semantics=sem, vmem_limit_bytes=VMEM_LIMIT_BYTES)


def _ffn_kernel(x_ref, *rest, merge, final_norm):
    rest = list(rest)
    x = x_ref[...]
    if merge:
        ya_ref, yb_ref, ga_ref, gb_ref, wa_ref, wb_ref, wo_ref = rest[:7]
        rest = rest[7:]
        pa = jnp.dot(ya_ref[...], wa_ref[...], preferred_element_type=F32)
        pb = jnp.dot(yb_ref[...], wb_ref[...], preferred_element_type=F32)
        merged = jax.nn.sigmoid(ga_ref[...].astype(F32)) * pa + jax.nn.sigmoid(gb_ref[...].astype(F32)) * pb
        x = x + jnp.dot(merged.astype(BF16), wo_ref[...], preferred_element_type=F32)
    g_ref, wg_ref, wu_ref, wd_ref = rest[:4]
    rest = rest[4:]
    if final_norm:
        fg_ref = rest.pop(0)
    o_ref, acc_ref = rest
    xn = _rms(x, g_ref[...]).astype(BF16)
    d_ff = wg_ref.shape[1]
    for c in range(0, d_ff, FFN_FCHUNK):
        gate = jnp.dot(xn, wg_ref[:, c:c + FFN_FCHUNK], preferred_element_type=F32)
        up = jnp.dot(xn, wu_ref[:, c:c + FFN_FCHUNK], preferred_element_type=F32)
        act = (gate * jax.nn.sigmoid(gate) * up).astype(BF16)
        part = jnp.dot(act, wd_ref[c:c + FFN_FCHUNK, :], preferred_element_type=F32)
        if c == 0:
            acc_ref[...] = part
        else:
            acc_ref[...] += part
    y = x + FFN_RESIDUAL_SCALE * acc_ref[...]
    if final_norm:
        y = _rms(y, fg_ref[...])
    o_ref[...] = y


def _ffn(x, norm_g, wg, wu, wd, final_g=None, merge=None):
    t, d = x.shape
    d_ff = wg.shape[1]
    rows = min(FFN_ROWS, t)
    row_spec = lambda n: pl.BlockSpec((rows, n), lambda i: (i, 0))
    in_specs = [row_spec(d)]
    args = [x]
    if merge is not None:
        in_specs += [row_spec(a.shape[1]) for a in merge[:4]] + [_const_spec(w.shape) for w in merge[4:]]
        args += list(merge)
    in_specs += [_const_spec((1, d)), _const_spec((d, d_ff)), _const_spec((d, d_ff)), _const_spec((d_ff, d))]
    args += [norm_g.reshape(1, d), wg, wu, wd]
    if final_g is not None:
        in_specs.append(_const_spec((1, d)))
        args.append(final_g.reshape(1, d))
    return pl.pallas_call(
        functools.partial(_ffn_kernel, merge=merge is not None, final_norm=final_g is not None),
        out_shape=jax.ShapeDtypeStruct((t, d), F32),
        grid=(t // rows,),
        in_specs=in_specs,
        out_specs=pl.BlockSpec((rows, d), lambda i: (i, 0)),
        scratch_shapes=[pltpu.VMEM((rows, d), F32)],
        compiler_params=_params("parallel"),
        name="merge_ffn" if merge is not None else "ffn",
    )(*args)


def _dot_nt(a, b):
    return lax.dot_general(a, b, (((1,), (1,)), ((), ())), preferred_element_type=F32)


def _in_proj_kernel(x_ref, g_ref, wz, wxbc, wk, wgs, wga, wmisc, wqt, wqit, wvt, wwit,
                    oz, oxbc, ok, ogs, oga, odt, oki, oqt, oqit, ovt, owit, *, seq):
    xn = _rms(x_ref[...], g_ref[...]).astype(BF16)
    for w_ref, o_ref in ((wz, oz), (wxbc, oxbc), (wgs, ogs), (wga, oga)):
        n = w_ref.shape[1]
        step = min(n, 512)
        for c in range(0, n, step):
            o_ref[:, c:c + step] = jnp.dot(xn, w_ref[:, c:c + step], preferred_element_type=F32).astype(o_ref.dtype)
    rows = x_ref.shape[0]
    pos = (pl.program_id(0) * rows) % seq + lax.broadcasted_iota(jnp.int32, (rows, KAUG_W), 0)
    lane = lax.broadcasted_iota(jnp.int32, (rows, KAUG_W), 1) - ATT_HEAD_DIM
    chunk_f = lax.shift_right_logical(pos, DSA_QBLOCK.bit_length() - 1).astype(F32)
    row_f = (pos & (DSA_QBLOCK - 1)).astype(F32)
    pos_cols = jnp.where((lane >= 0) & (lane < 3), chunk_f,
                         jnp.where((lane >= 3) & (lane < 6), row_f,
                                   jnp.where((lane >= 6) & (lane < 9), 1.0, 0.0)))
    k_nat = jnp.dot(xn, wk[...], preferred_element_type=F32) * LOG2E
    zeros = jnp.zeros((rows, KAUG_W - ATT_HEAD_DIM), F32)
    for g in range(ATT_KV_HEADS):
        k_g = jnp.concatenate([k_nat[:, g * ATT_HEAD_DIM:(g + 1) * ATT_HEAD_DIM], zeros], axis=1)
        ok[:, g * KAUG_W:(g + 1) * KAUG_W] = (k_g + pos_cols).astype(BF16)
    misc = jnp.dot(xn, wmisc[...], preferred_element_type=F32)
    odt[...] = misc[:, :SSM_HEADS]
    oki[...] = misc[:, SSM_HEADS:SSM_HEADS + IDX_DIM].astype(BF16)
    for w_ref, o_ref in ((wqt, oqt), (wqit, oqit)):
        n = w_ref.shape[0]
        for c in range(0, n, 256):
            o_ref[c:c + 256, :] = _dot_nt(w_ref[c:c + 256, :], xn).astype(o_ref.dtype)
    owit[...] = _dot_nt(wwit[...], xn)[:IDX_HEADS, :]
    vt = _dot_nt(wvt[...], xn).astype(BF16)
    for j in range(ovt.shape[0]):
        ovt[j] = vt[:, j * DSA_QBLOCK:(j + 1) * DSA_QBLOCK]


def _in_proj(x1, norm_g, w_in, seq):
    t, d = x1.shape
    offs = [0]
    for s in IN_SIZES:
        offs.append(offs[-1] + s)
    seg = lambda i: w_in[:, offs[i]:offs[i + 1]]
    w_misc = jnp.concatenate([seg(2), seg(7), jnp.zeros((d, 128 - SSM_HEADS - IDX_DIM), w_in.dtype)], axis=1)
    natural = [seg(0), seg(1), seg(4), seg(9), seg(10), w_misc]
    w_idx_t = jnp.concatenate([seg(8).T, jnp.zeros((16 - IDX_HEADS, d), w_in.dtype)], axis=0)
    transposed = [seg(3).T, seg(6).T, seg(5).T, w_idx_t]
    weights = [w.astype(BF16) for w in natural + transposed]
    rows = min(PROJ_ROWS, t)
    nchunk = rows // DSA_QBLOCK
    row_spec = lambda n: pl.BlockSpec((rows, n), lambda i: (i, 0))
    col_spec = lambda n: pl.BlockSpec((n, rows), lambda i: (0, i))
    kv_w = ATT_KV_HEADS * ATT_HEAD_DIM
    widths = [IN_SIZES[0], IN_SIZES[1], ATT_KV_HEADS * KAUG_W, IN_SIZES[9], IN_SIZES[10]]
    out_shapes = [jax.ShapeDtypeStruct((t, n), BF16) for n in widths]
    out_shapes += [jax.ShapeDtypeStruct((t, SSM_HEADS), F32), jax.ShapeDtypeStruct((t, IDX_DIM), BF16),
                   jax.ShapeDtypeStruct((ATT_HEADS * ATT_HEAD_DIM, t), BF16),
                   jax.ShapeDtypeStruct((IDX_HEADS * IDX_DIM, t), BF16),
                   jax.ShapeDtypeStruct((t // DSA_QBLOCK, kv_w, DSA_QBLOCK), BF16),
                   jax.ShapeDtypeStruct((IDX_HEADS, t), F32)]
    out_specs = [row_spec(s.shape[1]) for s in out_shapes[:7]]
    out_specs += [col_spec(ATT_HEADS * ATT_HEAD_DIM), col_spec(IDX_HEADS * IDX_DIM),
                  pl.BlockSpec((nchunk, kv_w, DSA_QBLOCK), lambda i: (i, 0, 0)), col_spec(IDX_HEADS)]
    return pl.pallas_call(
        functools.partial(_in_proj_kernel, seq=seq),
        out_shape=out_shapes,
        grid=(t // rows,),
        in_specs=[row_spec(d), _const_spec((1, d))] + [_const_spec(w.shape) for w in weights],
        out_specs=out_specs,
        compiler_params=_params("parallel"),
        name="in_proj",
    )(x1, norm_g.reshape(1, d), *weights)


def _ssd_kernel(xbc_ref, z_ref, dt_ref, shift_ref, convw_ref, convb_ref, dtb_ref, ahead2_ref, dexp_ref, normw_ref,
                o_ref, win_ref, state_ref):
    @pl.when(pl.program_id(1) == 0)
    def _():
        win_ref[0:CONV_HALO, :] = jnp.zeros((CONV_HALO, XBC_W), BF16)
        state_ref[...] = jnp.zeros_like(state_ref)

    for sub in range(xbc_ref.shape[0] // SSD_CHUNK):
        _ssd_chunk(slice(sub * SSD_CHUNK, (sub + 1) * SSD_CHUNK), xbc_ref, z_ref, dt_ref, shift_ref, convw_ref,
                   convb_ref, dtb_ref, ahead2_ref, dexp_ref, normw_ref, o_ref, win_ref, state_ref)


def _ssd_chunk(rows, xbc_ref, z_ref, dt_ref, shift_ref, convw_ref, convb_ref, dtb_ref, ahead2_ref, dexp_ref,
               normw_ref, o_ref, win_ref, state_ref):
    q = SSD_CHUNK
    hd, n, g = SSM_HEAD_DIM, SSM_STATE, SSM_GROUPS
    heads = SSM_HEADS
    hpg = heads // g
    gw = hpg * hd

    x_cur = xbc_ref[rows, :]
    win_ref[CONV_HALO:CONV_HALO + q, :] = x_cur
    pieces = []
    for c0 in range(0, XBC_W, SSD_CONV_COLS):
        cols = slice(c0, c0 + SSD_CONV_COLS)
        win = win_ref[:, cols].astype(F32)
        weighted = jnp.concatenate([(win * convw_ref[k:k + 1, cols]).astype(BF16) for k in range(CONV_K)], axis=0)
        conv = jnp.dot(shift_ref[...], weighted, preferred_element_type=F32) + convb_ref[:, cols]
        pieces.append(_silu(conv))
    win_ref[0:CONV_HALO, :] = x_cur[q - CONV_HALO:, :]
    xbc = jnp.concatenate(pieces, axis=1)
    xs = xbc[:, :D_INNER]
    bm = xbc[:, D_INNER:D_INNER + g * n]
    cm = xbc[:, D_INNER + g * n:]

    dt = jax.nn.softplus(dt_ref[rows, :] + dtb_ref[...])
    a2 = dt * ahead2_ref[...]
    row = lax.broadcasted_iota(jnp.int32, (q, q), 0)
    col = lax.broadcasted_iota(jnp.int32, (q, q), 1)
    causal = row >= col
    tril = jnp.where(causal, 1.0, 0.0).astype(BF16)
    a2_cs = _split_dot_left(tril, a2)
    a2_cs_t = _transpose_heads(a2_cs)
    a2_last = a2_cs[q - 1:q, :]
    stack = jnp.concatenate([dt, jnp.exp2(a2_cs), dt * jnp.exp2(a2_last - a2_cs),
                             jnp.broadcast_to(jnp.exp2(a2_last), (16, heads))], axis=0)
    hi = stack.astype(BF16)
    lo = (stack - hi.astype(F32)).astype(BF16)
    lane = lax.broadcasted_iota(jnp.int32, (2 * heads, heads * hd), 1)
    src = lax.broadcasted_iota(jnp.int32, (2 * heads, heads * hd), 0)
    first = jnp.where(src >= heads, src - heads, src) * hd
    expand = jnp.where((lane >= first) & (lane < first + hd), 1.0, 0.0).astype(BF16)
    ex = jnp.dot(jnp.concatenate([hi, lo], axis=1), expand, preferred_element_type=F32)
    dt_x, in_decay_x, w_state_x, chunk_decay_x = ex[0:q], ex[q:2 * q], ex[2 * q:3 * q], ex[3 * q:3 * q + 1]

    prev = state_ref[...]
    prev_b = prev.astype(BF16)
    lane_pair = lax.broadcasted_iota(jnp.int32, (q, 2 * hd), 1)
    ys = []
    for gi in range(g):
        sl = slice(gi * gw, (gi + 1) * gw)
        bg = bm[:, gi * n:(gi + 1) * n]
        bg_b = bg.astype(BF16)
        cg_b = cm[:, gi * n:(gi + 1) * n].astype(BF16)
        cb = jnp.where(causal, _dot_nt(cg_b, bg_b), 0.0)
        xs_g = xs[:, sl]
        xdt_b = (xs_g * dt_x[:, sl]).astype(BF16)
        y_off = jnp.dot(cg_b, prev_b[:, sl], preferred_element_type=F32)
        pairs = []
        for pr in range(hpg // 2):
            ms = []
            for h in (gi * hpg + 2 * pr, gi * hpg + 2 * pr + 1):
                seg = jnp.broadcast_to(a2_cs[:, h:h + 1], (q, q)) - a2_cs_t[h:h + 1, :]
                ms.append((cb * jnp.exp2(jnp.minimum(seg, 0.0))).astype(BF16))
            xp = xdt_b[:, 2 * pr * hd:(2 * pr + 2) * hd]
            zero = jnp.zeros_like(xp)
            rhs = jnp.concatenate([jnp.where(lane_pair < hd, xp, zero), jnp.where(lane_pair >= hd, xp, zero)], axis=0)
            pairs.append(jnp.dot(jnp.concatenate(ms, axis=1), rhs, preferred_element_type=F32))
        ys.append(jnp.concatenate(pairs, axis=1) + y_off * in_decay_x[:, sl] + dexp_ref[:, sl] * xs_g)
        xw = (xs_g * w_state_x[:, sl]).astype(BF16)
        s_new = jnp.dot(bg.T.astype(BF16), xw, preferred_element_type=F32)
        state_ref[:, sl] = prev[:, sl] * chunk_decay_x[:, sl] + s_new
    y = jnp.concatenate(ys, axis=1)

    z = z_ref[rows, :].astype(F32)
    yg = y * _silu(z)
    outs = []
    for gi in range(g):
        blk = yg[:, gi * gw:(gi + 1) * gw]
        ms = jnp.mean(blk * blk, axis=-1, keepdims=True)
        outs.append(blk * lax.rsqrt(ms + EPS))
    o_ref[rows, :] = (jnp.concatenate(outs, axis=1) * normw_ref[...]).astype(o_ref.dtype)


def _transpose_heads(x):
    q, h = x.shape
    return jnp.concatenate([x, jnp.zeros((q, q - h), x.dtype)], axis=1).T[:h, :]


def _split_dot_left(b01, a):
    out = None
    rem = a
    for _ in range(3):
        hi = rem.astype(BF16)
        term = jnp.dot(b01, hi, preferred_element_type=F32)
        out = term if out is None else out + term
        rem = rem - hi.astype(F32)
    return out


def _conv_shift_matrix(q):
    s = np.zeros((q, CONV_K * (CONV_HALO + q)), np.float32)
    for k in range(CONV_K):
        s[np.arange(q), k * (CONV_HALO + q) + CONV_HALO + np.arange(q) - (CONV_K - 1) + k] = 1.0
    return jnp.asarray(s, BF16)


def _ssd(xbc, z, dt_raw, conv_w, conv_b, dt_bias, a_log, d_skip, norm_w, bsz, seq):
    q = SSD_CHUNK
    step_rows = q * min(SSD_SUB, seq // q)
    nc = seq // step_rows
    a_head2 = (-jnp.exp(a_log.astype(F32)) * LOG2E).reshape(1, SSM_HEADS)
    d_exp = jnp.repeat(d_skip.astype(F32), SSM_HEAD_DIM).reshape(1, D_INNER)
    shift = _conv_shift_matrix(q)
    row_spec = lambda n: pl.BlockSpec((step_rows, n), lambda b, c: (b * nc + c, 0))
    return pl.pallas_call(
        _ssd_kernel,
        out_shape=jax.ShapeDtypeStruct((bsz * seq, D_INNER), BF16),
        grid=(bsz, nc),
        in_specs=[row_spec(XBC_W), row_spec(D_INNER), row_spec(SSM_HEADS), _const_spec(shift.shape),
                  _const_spec((CONV_K, XBC_W)), _const_spec((1, XBC_W)), _const_spec((1, SSM_HEADS)),
                  _const_spec((1, SSM_HEADS)), _const_spec((1, D_INNER)), _const_spec((1, D_INNER))],
        out_specs=row_spec(D_INNER),
        scratch_shapes=[pltpu.VMEM((CONV_HALO + q, XBC_W), BF16), pltpu.VMEM((SSM_STATE, D_INNER), F32)],
        compiler_params=_params("parallel", "arbitrary"),
        name="ssd",
    )(xbc, z, dt_raw, shift, conv_w.astype(F32), conv_b.reshape(1, XBC_W).astype(F32),
      dt_bias.reshape(1, SSM_HEADS).astype(F32), a_head2, d_exp, norm_w.reshape(1, D_INNER).astype(F32))


def _sortable_to_f32(s):
    bits = jnp.where(s < 0, s ^ jnp.int32(-2 ** 31), ~s)
    return lax.bitcast_convert_type(bits, F32)


def _tile_count(mask):
    ones = jnp.where(mask, 1.0, 0.0)
    return jnp.sum(ones.reshape(ones.shape[0] // 8, 8, ones.shape[1]), axis=0)


def _np_bf16_pieces(x, n=3):
    out = []
    rem = np.float32(x)
    for _ in range(n):
        piece = np.float32(np.asarray(rem, dtype=jnp.bfloat16))
        out.append(piece)
        rem = np.float32(rem - piece)
    return out


def _bf16_pieces(x, n=3):
    out = []
    rem = x
    for _ in range(n):
        piece = rem.astype(BF16).astype(F32)
        out.append(piece)
        rem = rem - piece
    return out


def _dsa_kernel(qa_ref, qb_ref, qia_ref, qib_ref, wa_ref, wb_ref, k_ref, ki_ref, vt_ref, oa_ref, ob_ref,
                sc_ref, mask_ref, m_ref, acc_ref, qaug_ref, qis_ref, ws_ref, rel0_ref, rel1_ref, raw0_ref, raw1_ref,
                *, top_k, nq):
    tq = DSA_QBLOCK
    rep = ATT_HEADS // ATT_KV_HEADS
    hd = ATT_HEAD_DIM
    p = pl.program_id(1)
    n_a = p + 1
    nslot = nq + 1
    t_a = p * tq + lax.broadcasted_iota(jnp.int32, (1, tq), 1)
    t_b = (nq - 1 - p) * tq + lax.broadcasted_iota(jnp.int32, (1, tq), 1)
    key_in_tile = lax.broadcasted_iota(jnp.int32, (tq, tq), 0)
    kf = float(top_k)

    def slot_info(s):
        is_a = s <= p
        return jnp.where(is_a, 0, 1), jnp.minimum(jnp.where(is_a, s, s - n_a), nq - 1)

    for blk, (qi_ref, w_ref, q_ref, t_row) in enumerate(((qia_ref, wa_ref, qa_ref, t_a), (qib_ref, wb_ref, qb_ref, t_b))):
        qi = qi_ref[...]
        qis_ref[blk] = jnp.concatenate([qi[j * IDX_DIM:(j + 1) * IDX_DIM, :] for j in range(IDX_HEADS)], axis=1)
        ws_ref[blk] = w_ref[...] * (IDX_HEADS ** -0.5) * (IDX_DIM ** -0.5)
        row16 = lax.broadcasted_iota(jnp.int32, (16, tq), 0)
        for g in range(ATT_KV_HEADS):
            q_t = jnp.concatenate([q_ref[(g * rep + r) * hd:(g * rep + r + 1) * hd, :] for r in range(rep)], axis=1)
            bias_rows = []
            for r in range(rep):
                slope = np.float32(2.0 ** (-ALIBI_BASE * (g * rep + r + 1) / ATT_HEADS) * LOG2E)
                pieces = [float(v) for v in _np_bf16_pieces(slope)]
                terms = [tq * v for v in pieces] + pieces + _bf16_pieces(-(slope * t_row.astype(F32)))
                rows = jnp.zeros((16, tq), F32)
                for i, term in enumerate(terms):
                    rows = jnp.where(row16 == i, term, rows)
                bias_rows.append(rows)
            qaug_ref[blk * ATT_KV_HEADS + g] = jnp.concatenate(
                [q_t * (hd ** -0.5), jnp.concatenate(bias_rows, axis=1).astype(BF16),
                 jnp.zeros((KAUG_W - hd - 16, rep * tq), BF16)], axis=0)

    def issue_rel(s, rel_ref):
        blk, c = slot_info(s)
        kc = ki_ref[pl.ds(pl.multiple_of(c * tq, tq), tq), :]
        rel_ref[...] = jnp.dot(kc, qis_ref[blk], preferred_element_type=F32)

    def score_slot(s, rel_ref):
        blk, c = slot_info(s)
        w = ws_ref[blk]
        acc = w[0:1, :] * jnp.maximum(rel_ref[:, 0:tq], 0.0)
        for j in range(1, IDX_HEADS):
            acc = acc + w[j:j + 1, :] * jnp.maximum(rel_ref[:, j * tq:(j + 1) * tq], 0.0)
        sc_ref[s] = jnp.where(c * tq + key_in_tile <= jnp.where(blk == 0, t_a, t_b), acc, -jnp.inf)

    issue_rel(0, rel0_ref)

    def score_trip(j, carry):
        s0 = 2 * j
        s1 = jnp.minimum(s0 + 1, nslot - 1)
        issue_rel(s1, rel1_ref)
        score_slot(s0, rel0_ref)
        issue_rel(jnp.minimum(s0 + 2, nslot - 1), rel0_ref)
        score_slot(s1, rel1_ref)
        return carry

    lax.fori_loop(0, (nslot + 1) // 2, score_trip, 0, unroll=True)

    def both_counts(pred):
        acc_a = jnp.zeros((8, tq), F32)
        acc_b = jnp.zeros((8, tq), F32)
        for s in range(nslot):
            is_a = s <= p
            part = _tile_count(pred(s, sc_ref[s], is_a))
            if s == 0:
                acc_a = acc_a + part
            elif s >= nq // 2:
                acc_b = acc_b + part
            else:
                acc_a = acc_a + jnp.where(is_a, part, 0.0)
                acc_b = acc_b + jnp.where(is_a, 0.0, part)
        return jnp.sum(acc_a, axis=0, keepdims=True), jnp.sum(acc_b, axis=0, keepdims=True)

    def value_step(b, carry):
        pre_a, pre_b = carry
        bit = lax.shift_left(jnp.int32(1), 31 - b)
        cand_a, cand_b = pre_a | bit, pre_b | bit
        cf_a, cf_b = _sortable_to_f32(cand_a), _sortable_to_f32(cand_b)
        cnt_a, cnt_b = both_counts(lambda s, tile, is_a: tile >= jnp.where(is_a, cf_a, cf_b))
        return jnp.where(cnt_a >= kf, cand_a, pre_a), jnp.where(cnt_b >= kf, cand_b, pre_b)

    zero_row = jnp.zeros((1, tq), jnp.int32)
    pre_a, pre_b = lax.fori_loop(0, 32, value_step, (zero_row, zero_row), unroll=4)
    thr_a, thr_b = _sortable_to_f32(pre_a), _sortable_to_f32(pre_b)
    above_a, above_b = both_counts(lambda s, tile, is_a: tile > jnp.where(is_a, thr_a, thr_b))
    need_a, need_b = kf - above_a, kf - above_b

    def slot_key(s, is_a):
        return jnp.where(is_a, s, s - n_a) * tq + key_in_tile

    def index_step(b, carry):
        lo_a, lo_b = carry
        bit = lax.shift_left(jnp.int32(1), b)
        cand_a, cand_b = lo_a + bit, lo_b + bit
        cnt_a, cnt_b = both_counts(
            lambda s, tile, is_a: (tile == jnp.where(is_a, thr_a, thr_b))
            & (slot_key(s, is_a) < jnp.where(is_a, cand_a, cand_b)))
        return jnp.where(cnt_a < need_a, cand_a, lo_a), jnp.where(cnt_b < need_b, cand_b, lo_b)

    tie_a, tie_b = both_counts(lambda s, tile, is_a: tile == jnp.where(is_a, thr_a, thr_b))
    excess = jnp.maximum(jnp.max(tie_a - need_a), jnp.max(tie_b - need_b))
    nbits = (nq * tq - 1).bit_length()
    all_keys = jnp.full((1, tq), nq * tq, jnp.int32)
    last_a, last_b = lax.cond(
        excess > 0.0,
        lambda: lax.fori_loop(0, nbits, lambda b, c: index_step(nbits - 1 - b, c), (zero_row, zero_row)),
        lambda: (all_keys, all_keys))

    def finish(t_row, thr, last):
        early = t_row < top_k
        return jnp.where(early, -jnp.finfo(F32).max, thr), jnp.where(early, nq * tq, last)

    thr_a, last_a = finish(t_a, thr_a, last_a)
    thr_b, last_b = finish(t_b, thr_b, last_b)

    n_b = nq - p
    a_slots = n_a + (n_a & 1)

    def att_slot_info(s):
        is_a = s < a_slots
        local = jnp.where(is_a, s, s - a_slots)
        count = jnp.where(is_a, n_a, n_b)
        chunk = jnp.minimum(local, count - 1)
        return jnp.where(is_a, 0, 1), chunk, jnp.where(is_a, chunk, n_a + chunk), local >= count

    for s in range(nslot + 1):
        blk, chunk, src, spare = att_slot_info(s)
        is_a = blk == 0
        thr = jnp.where(is_a, thr_a, thr_b)
        tile = sc_ref[src]
        key = chunk * tq + key_in_tile
        sel = (tile > thr) | ((tile == thr) & (key <= jnp.where(is_a, last_a, last_b)))
        mask_ref[s] = jnp.where(sel & jnp.logical_not(spare), 0.0, MASK_NEG)

    ones_rows = jnp.where(lax.broadcasted_iota(jnp.int32, (16, 2 * tq), 0) == 0, 1.0, 0.0).astype(BF16)
    m_ref[...] = jnp.full(m_ref.shape, -jnp.inf, F32)
    acc_ref[...] = jnp.zeros(acc_ref.shape, F32)

    def issue_qk(pair, g, raw_ref):
        for u in range(2):
            blk, c, _, _ = att_slot_info(2 * pair + u)
            kc = k_ref[pl.ds(pl.multiple_of(c * tq, tq), tq), g * KAUG_W:(g + 1) * KAUG_W]
            raw_ref[u, g] = jnp.dot(kc, qaug_ref[blk * ATT_KV_HEADS + g], preferred_element_type=F32)

    def attend_group(pair, g, raw_ref):
        blk, c0, _, _ = att_slot_info(2 * pair)
        _, c1, _, _ = att_slot_info(2 * pair + 1)
        slot = blk * ATT_KV_HEADS + g
        m_old = m_ref[slot]
        mask = mask_ref[2 * pair]
        logits = jnp.concatenate([raw_ref[0, g, :, r * tq:(r + 1) * tq] + mask for r in range(rep)], axis=1)
        m_mid = jnp.maximum(m_old, jnp.max(logits, axis=0, keepdims=True))
        prob0 = jnp.exp2(logits - m_mid).astype(BF16)
        mask = mask_ref[2 * pair + 1]
        logits = jnp.concatenate([raw_ref[1, g, :, r * tq:(r + 1) * tq] + mask for r in range(rep)], axis=1)
        m_new = jnp.maximum(m_mid, jnp.max(logits, axis=0, keepdims=True))
        prob1 = jnp.exp2(logits - m_new).astype(BF16)
        prob = jnp.concatenate([prob0 * jnp.exp2(m_mid - m_new).astype(BF16), prob1], axis=0)
        v_pair = jnp.concatenate([vt_ref[c0, g * hd:(g + 1) * hd, :], vt_ref[c1, g * hd:(g + 1) * hd, :]], axis=1)
        v_aug = jnp.concatenate([v_pair, ones_rows], axis=0)
        pv = jnp.dot(v_aug, prob, preferred_element_type=F32)
        acc_ref[slot] = acc_ref[slot] * jnp.exp2(m_old - m_new) + pv
        m_ref[slot] = m_new

    npair = (nslot + 1) // 2
    for g in range(ATT_KV_HEADS):
        issue_qk(0, g, raw0_ref)
    for j in range(npair):
        cur, nxt = (raw0_ref, raw1_ref) if j % 2 == 0 else (raw1_ref, raw0_ref)
        for g in range(ATT_KV_HEADS):
            if j + 1 < npair:
                issue_qk(j + 1, g, nxt)
            attend_group(j, g, cur)

    for blk, o_ref in enumerate((oa_ref, ob_ref)):
        outs = []
        for g in range(ATT_KV_HEADS):
            acc = acc_ref[blk * ATT_KV_HEADS + g]
            o_t = acc[:hd, :] / acc[hd:hd + 1, :]
            outs.extend(o_t[:, r * tq:(r + 1) * tq] for r in range(rep))
        o_ref[...] = jnp.concatenate(outs, axis=0).T.astype(o_ref.dtype)


def _dsa(q_t, k, v_t, qi_t, k_idx, wi_t, bsz, seq):
    tq = DSA_QBLOCK
    nq = seq // tq
    half = nq // 2
    top_k = min(TOPK_MAX, seq // 4)
    width = ATT_HEADS * ATT_HEAD_DIM
    gq = (ATT_HEADS // ATT_KV_HEADS) * tq
    col_a = lambda n: pl.BlockSpec((n, tq), lambda b, p: (0, b * nq + p))
    col_b = lambda n: pl.BlockSpec((n, tq), lambda b, p: (0, b * nq + nq - 1 - p))
    full = lambda n: pl.BlockSpec((seq, n), lambda b, p: (b, 0))
    out_a, out_b = pl.pallas_call(
        functools.partial(_dsa_kernel, top_k=top_k, nq=nq),
        out_shape=[jax.ShapeDtypeStruct((bsz, half * tq, width), BF16)] * 2,
        grid=(bsz, half),
        in_specs=[col_a(q_t.shape[0]), col_b(q_t.shape[0]), col_a(qi_t.shape[0]), col_b(qi_t.shape[0]),
                  col_a(wi_t.shape[0]), col_b(wi_t.shape[0]), full(k.shape[1]), full(k_idx.shape[1]),
                  pl.BlockSpec((nq, v_t.shape[1], tq), lambda b, p: (b, 0, 0))],
        out_specs=[pl.BlockSpec((None, tq, width), lambda b, p: (b, p, 0)),
                   pl.BlockSpec((None, tq, width), lambda b, p: (b, half - 1 - p, 0))],
        scratch_shapes=[pltpu.VMEM((nq + 1, tq, tq), F32),
                        pltpu.VMEM((nq + 2, tq, tq), F32),
                        pltpu.VMEM((2 * ATT_KV_HEADS, 1, gq), F32),
                        pltpu.VMEM((2 * ATT_KV_HEADS, ATT_HEAD_DIM + 16, gq), F32),
                        pltpu.VMEM((2 * ATT_KV_HEADS, KAUG_W, gq), BF16),
                        pltpu.VMEM((2, IDX_DIM, IDX_HEADS * tq), BF16),
                        pltpu.VMEM((2, IDX_HEADS, tq), F32),
                        pltpu.VMEM((tq, IDX_HEADS * tq), F32), pltpu.VMEM((tq, IDX_HEADS * tq), F32),
                        pltpu.VMEM((2, ATT_KV_HEADS, tq, gq), F32),
                        pltpu.VMEM((2, ATT_KV_HEADS, tq, gq), F32)],
        compiler_params=_params("parallel", "arbitrary"),
        name="dsa",
    )(q_t, q_t, qi_t, qi_t, wi_t, wi_t, k, k_idx, v_t)
    return jnp.concatenate([out_a, out_b], axis=1).reshape(bsz * seq, width)


def kernel(x, ffn1_norm, ffn1_w_gate, ffn1_w_up, ffn1_w_down, mix_norm, w_in, conv_w, conv_b, dt_bias, a_log,
           d_skip, ssm_norm, w_branch_ssm, w_branch_attn, w_out, ffn2_norm, ffn2_w_gate, ffn2_w_up, ffn2_w_down,
           final_norm):
    bsz, seq, d = x.shape
    depth = ffn1_norm.shape[0]
    h = x.reshape(bsz * seq, d)
    for l in range(depth):
        last = l == depth - 1
        h = _ffn(h, ffn1_norm[l], ffn1_w_gate[l].astype(BF16), ffn1_w_up[l].astype(BF16),
                 ffn1_w_down[l].astype(BF16))
        z, xbc, k, g_ssm, g_att, dt_raw, k_idx, q_t, qi_t, v_t, wi_t = _in_proj(h, mix_norm[l], w_in[l], seq)
        y_ssm = _ssd(xbc, z, dt_raw, conv_w[l], conv_b[l], dt_bias[l], a_log[l], d_skip[l], ssm_norm[l], bsz, seq)
        y_att = _dsa(q_t, k, v_t, qi_t, k_idx, wi_t, bsz, seq)
        merge = (y_ssm, y_att, g_ssm, g_att, w_branch_ssm[l].astype(BF16), w_branch_attn[l].astype(BF16),
                 w_out[l].astype(BF16))
        h = _ffn(h, ffn2_norm[l], ffn2_w_gate[l].astype(BF16), ffn2_w_up[l].astype(BF16),
                 ffn2_w_down[l].astype(BF16), final_g=final_norm if last else None, merge=merge)
    return h.reshape(bsz, seq, d)
```
